```python
import functools
import jax, jax.numpy as jnp
from jax import lax
import numpy as np

D_MODEL = 1024
BATCH = 4
SEQ = 8192
DEPTH = 1
DEC_BATCH = 128
DEC_SEQ = 1
PAST_LEN = 8192
PAGE_SIZE = 128

GLA_HEADS = 4
GLA_DK = 128
GLA_DV = 256
GLA_GATE_RANK = 16
GLA_TAU = 16.0
GLA_CHUNK = 64
ATT_HEADS = 8
ATT_DH = 128
IDX_HEADS = 8
IDX_DH = 64
TOPK_MAX = 256
Q_BLOCK = 64
ROPE_THETA = 10000.0
PEER_HEADS = 8
PEER_NKEYS = 128
PEER_N = PEER_NKEYS * PEER_NKEYS
PEER_DKEY = 128
PEER_TOPK = 16
PEER_BLOCK = 128
EPS = 1e-6

GLA_QK_W = GLA_HEADS * GLA_DK
GLA_V_W = GLA_HEADS * GLA_DV
ATT_W = ATT_HEADS * ATT_DH
IDX_Q_W = IDX_HEADS * IDX_DH
COL_SPLITS = (GLA_QK_W, GLA_QK_W, GLA_V_W, GLA_V_W, GLA_GATE_RANK,
              ATT_W, ATT_W, ATT_W, IDX_Q_W, IDX_DH, IDX_HEADS, D_MODEL, D_MODEL)
D_IN = GLA_QK_W * 2 + GLA_V_W * 2 + GLA_GATE_RANK + ATT_W * 3 + IDX_Q_W + IDX_DH + IDX_HEADS + D_MODEL * 2

kernel_name = "hybrid_gla_dsa_peer_step"


def rmsnorm(x, g):
    xf = x.astype(jnp.float32)
    y = xf * lax.rsqrt(jnp.mean(xf * xf, axis=-1, keepdims=True) + EPS)
    return (y * g.astype(jnp.float32)).astype(x.dtype)


def rope(x, pos):
    half = x.shape[-1] // 2
    inv = ROPE_THETA ** (-jnp.arange(half, dtype=jnp.float32) / half)
    ang = pos.astype(jnp.float32)[:, None] * inv[None, :]
    cos = jnp.cos(ang)[:, None, :]
    sin = jnp.sin(ang)[:, None, :]
    xf = x.astype(jnp.float32)
    x1, x2 = xf[..., :half], xf[..., half:]
    return jnp.concatenate([x1 * cos - x2 * sin, x2 * cos + x1 * sin], axis=-1).astype(x.dtype)


def gla_chunked(q, k, v, log_a, s0):
    B, T, H, dk = q.shape
    dv = v.shape[-1]
    C = GLA_CHUNK if T % GLA_CHUNK == 0 else T
    n = T // C

    def to_chunks(a):
        return a.reshape(B, n, C, H, a.shape[-1]).transpose(1, 0, 3, 2, 4)

    qc, kc, vc, ac = to_chunks(q), to_chunks(k), to_chunks(v), to_chunks(log_a)
    causal = jnp.tril(jnp.ones((C, C), dtype=bool))

    def step(S, xs):
        qi, ki, vi, ai = xs
        b = jnp.cumsum(ai.astype(jnp.float32), axis=-2)
        kf = ki.astype(jnp.float32)
        vf = vi.astype(jnp.float32)
        q_dec = qi.astype(jnp.float32) * jnp.exp(b)
        A = jnp.einsum('bhid,bhjd->bhij', q_dec, kf * jnp.exp(-b))
        A = jnp.where(causal, A, 0.0)
        o = jnp.einsum('bhij,bhjv->bhiv', A, vf) + jnp.einsum('bhik,bhkv->bhiv', q_dec, S)
        b_last = b[..., -1:, :]
        S_new = jnp.exp(b_last[..., 0, :])[..., None] * S + jnp.einsum('bhjk,bhjv->bhkv', kf * jnp.exp(b_last - b), vf)
        return S_new, o

    S, o = lax.scan(step, s0.astype(jnp.float32), (qc, kc, vc, ac))
    o = o.transpose(1, 0, 3, 2, 4).reshape(B, T, H, dv)
    return o.astype(v.dtype), S.astype(s0.dtype)


def index_scores(iq, ik, iw):
    rel = jax.nn.relu(jnp.einsum('bthd,bsd->bths', iq, ik))
    return jnp.einsum('bths,bth->bts', rel, iw).astype(jnp.float32)


def sparse_attend(q, kg, vg, valid):
    s = jnp.einsum('bthd,btkhd->bthk', q, kg).astype(jnp.float32) * (ATT_DH ** -0.5)
    s = jnp.where(valid[:, :, None, :], s, -jnp.inf)
    p = jax.nn.softmax(s, axis=-1).astype(vg.dtype)
    return jnp.einsum('bthk,btkhd->bthd', p, vg)


def dsa_prompt(q, k, v, iq, ik, iw):
    B, S, H, dh = q.shape
    topk = min(TOPK_MAX, S // 4)
    nb = S // Q_BLOCK
    key_pos = jnp.arange(S)

    def blk(a):
        return a.reshape(B, nb, Q_BLOCK, *a.shape[2:]).swapaxes(0, 1)

    def one(xs):
        qb, iqb, iwb, pos = xs
        isc = index_scores(iqb, ik, iwb)
        isc = jnp.where(key_pos[None, None, :] <= pos[None, :, None], isc, -jnp.inf)
        _, sel = lax.top_k(isc, topk)
        kg = jax.vmap(lambda kk, ii: kk[ii])(k, sel)
        vg = jax.vmap(lambda vv, ii: vv[ii])(v, sel)
        return sparse_attend(qb, kg, vg, sel <= pos[None, :, None])

    o = lax.map(one, (blk(q), blk(iq), blk(iw), key_pos.reshape(nb, Q_BLOCK)))
    return o.swapaxes(0, 1).reshape(B, S, H, dh)


def dsa_sample(q, k, v, iq, ik, iw, cache_k, cache_v, cache_idx_k, page_table, layer):
    Bd, T, H, dh = q.shape
    n_pages = page_table.shape[1]
    past = n_pages * PAGE_SIZE
    L = past + T
    topk = min(TOPK_MAX, L // 4)
    ik_past = cache_idx_k[layer, page_table].reshape(Bd, past, IDX_DH)
    ik_all = jnp.concatenate([ik_past, ik], axis=1)
    qpos = past + jnp.arange(T)
    isc = index_scores(iq, ik_all, iw)
    isc = jnp.where(jnp.arange(L)[None, None, :] <= qpos[None, :, None], isc, -jnp.inf)
    _, sel = lax.top_k(isc, topk)
    in_past = sel < past
    ps = jnp.minimum(sel, past - 1)
    phys = page_table[jnp.arange(Bd)[:, None, None], ps // PAGE_SIZE]
    slot = ps % PAGE_SIZE
    ns = jnp.clip(sel - past, 0, T - 1)
    k_new = jax.vmap(lambda kk, ii: kk[ii])(k, ns)
    v_new = jax.vmap(lambda vv, ii: vv[ii])(v, ns)
    kg = jnp.where(in_past[..., None, None], cache_k[layer, phys, slot], k_new)
    vg = jnp.where(in_past[..., None, None], cache_v[layer, phys, slot], v_new)
    return sparse_attend(q, kg, vg, sel <= qpos[None, :, None])


def peer_ffn(h, peer_wq, peer_keys, peer_u, peer_v):
    shape = h.shape
    x = h.reshape(-1, D_MODEL)
    n = x.shape[0]
    pad = (-n) % PEER_BLOCK
    x = jnp.pad(x, ((0, pad), (0, 0)))

    def one(xb):
        qh = (xb @ peer_wq).reshape(-1, PEER_HEADS, 2, PEER_DKEY // 2)
        s = jnp.einsum('nhpd,hpkd->nhpk', qh, peer_keys).astype(jnp.float32)
        s1, i1 = lax.top_k(s[:, :, 0], PEER_TOPK)
        s2, i2 = lax.top_k(s[:, :, 1], PEER_TOPK)
        cand = (s1[..., :, None] + s2[..., None, :]).reshape(-1, PEER_HEADS, PEER_TOPK * PEER_TOPK)
        cid = (i1[..., :, None] * PEER_NKEYS + i2[..., None, :]).reshape(-1, PEER_HEADS, PEER_TOPK * PEER_TOPK)
        top, j = lax.top_k(cand, PEER_TOPK)
        eid = jnp.take_along_axis(cid, j, axis=-1)
        g = jax.nn.softmax(top, axis=-1)
        u = peer_u[eid]
        vv = peer_v[eid]
        a = jax.nn.gelu(jnp.einsum('nd,nhkd->nhk', xb, u), approximate=False)
        return jnp.einsum('nhk,nhkd->nd', (g * a.astype(jnp.float32)).astype(xb.dtype), vv)

    y = lax.map(one, x.reshape(-1, PEER_BLOCK, D_MODEL)).reshape(-1, D_MODEL)[:n]
    return y.reshape(shape)


def trunk_layer(x, pos, gla_s0, attend, norm_mix, w_in, gla_a2, gla_a_bias, q_norm, k_norm, gla_norm,
                w_br_gla, w_br_att, w_out, norm_ffn, peer_wq, peer_keys, peer_u, peer_v):
    B, T, _ = x.shape
    h = rmsnorm(x, norm_mix)
    z = h @ w_in
    offsets = np.cumsum(COL_SPLITS)[:-1].tolist()
    gq, gk, gv, gr, glr, aq, ak, av, iq, ik, iw, ga, gb = jnp.split(z, offsets, axis=-1)
    gq = gq.reshape(B, T, GLA_HEADS, GLA_DK) * (GLA_DK ** -0.5)
    gk = gk.reshape(B, T, GLA_HEADS, GLA_DK)
    gv = gv.reshape(B, T, GLA_HEADS, GLA_DV)
    log_a = jax.nn.log_sigmoid((glr @ gla_a2 + gla_a_bias).astype(jnp.float32)).reshape(B, T, GLA_HEADS, GLA_DK) / GLA_TAU
    o_gla, s_gla = gla_chunked(gq, gk, gv, log_a, gla_s0)
    o_gla = rmsnorm(o_gla, gla_norm).reshape(B, T, GLA_V_W) * jax.nn.silu(gr)
    aq = rope(rmsnorm(aq.reshape(B, T, ATT_HEADS, ATT_DH), q_norm), pos)
    ak = rope(rmsnorm(ak.reshape(B, T, ATT_HEADS, ATT_DH), k_norm), pos)
    av = av.reshape(B, T, ATT_HEADS, ATT_DH)
    iq = rope(iq.reshape(B, T, IDX_HEADS, IDX_DH), pos) * (IDX_DH ** -0.5)
    ik = rope(ik[:, :, None, :], pos)[:, :, 0, :]
    iw = iw * (IDX_HEADS ** -0.5)
    o_att = attend(aq, ak, av, iq, ik, iw).reshape(B, T, ATT_W)
    mix = jax.nn.sigmoid(ga) * (o_gla @ w_br_gla) + jax.nn.sigmoid(gb) * (o_att @ w_br_att)
    x = x + mix @ w_out
    x = x + peer_ffn(rmsnorm(x, norm_ffn), peer_wq, peer_keys, peer_u, peer_v)
    return x, ak, av, ik, s_gla


def setup_inputs(seed: int = 0) -> dict:
    key = jax.random.key(seed)
    ks = jax.random.split(key, 32)
    f32 = jnp.float32
    n_pages = PAST_LEN // PAGE_SIZE
    n_used = DEC_BATCH * n_pages
    n_phys = n_used + n_used // 4

    def nrm(k, shape, scale):
        return jax.random.normal(k, shape, f32) * scale

    return {
        "x_prompt": nrm(ks[0], (BATCH, SEQ, D_MODEL), 1.0),
        "x_sample": nrm(ks[1], (DEC_BATCH, DEC_SEQ, D_MODEL), 1.0),
        "cache_k": nrm(ks[2], (DEPTH, n_phys, PAGE_SIZE, ATT_HEADS, ATT_DH), 1.0),
        "cache_v": nrm(ks[3], (DEPTH, n_phys, PAGE_SIZE, ATT_HEADS, ATT_DH), 1.0),
        "cache_idx_k": nrm(ks[4], (DEPTH, n_phys, PAGE_SIZE, IDX_DH), 1.0),
        "state_gla": nrm(ks[5], (DEPTH, DEC_BATCH, GLA_HEADS, GLA_DK, GLA_DV), 0.1),
        "page_table": jax.random.permutation(ks[6], n_phys)[:n_used].reshape(DEC_BATCH, n_pages).astype(jnp.int32),
        "norm_mix": 1.0 + nrm(ks[7], (DEPTH, D_MODEL), 0.02),
        "w_in": nrm(ks[8], (DEPTH, D_MODEL, D_IN), D_MODEL ** -0.5),
        "gla_a2": nrm(ks[9], (DEPTH, GLA_GATE_RANK, GLA_QK_W), GLA_GATE_RANK ** -0.5),
        "gla_a_bias": nrm(ks[10], (DEPTH, GLA_QK_W), 0.1),
        "q_norm": 1.0 + nrm(ks[11], (DEPTH, ATT_DH), 0.02),
        "k_norm": 1.0 + nrm(ks[12], (DEPTH, ATT_DH), 0.02),
        "gla_norm": 1.0 + nrm(ks[13], (DEPTH, GLA_DV), 0.02),
        "w_br_gla": nrm(ks[14], (DEPTH, GLA_V_W, D_MODEL), GLA_V_W ** -0.5),
        "w_br_att": nrm(ks[15], (DEPTH, ATT_W, D_MODEL), ATT_W ** -0.5),
        "w_out": nrm(ks[16], (DEPTH, D_MODEL, D_MODEL), D_MODEL ** -0.5),
        "norm_ffn": 1.0 + nrm(ks[17], (DEPTH, D_MODEL), 0.02),
        "peer_wq": nrm(ks[18], (DEPTH, D_MODEL, PEER_HEADS * PEER_DKEY), D_MODEL ** -0.5),
        "peer_keys": nrm(ks[19], (DEPTH, PEER_HEADS, 2, PEER_NKEYS, PEER_DKEY // 2), (PEER_DKEY // 2) ** -0.5),
        "peer_u": nrm(ks[20], (DEPTH, PEER_N, D_MODEL), D_MODEL ** -0.5),
        "peer_v": nrm(ks[21], (DEPTH, PEER_N, D_MODEL), 0.5),
    }


def reference(x_prompt, x_sample, cache_k, cache_v, cache_idx_k, state_gla, page_table, norm_mix, w_in,
              gla_a2, gla_a_bias, q_norm, k_norm, gla_norm, w_br_gla, w_br_att, w_out, norm_ffn,
              peer_wq, peer_keys, peer_u, peer_v):
    B = x_prompt.shape[0]
    past = page_table.shape[1] * PAGE_SIZE
    pos_p = jnp.arange(x_prompt.shape[1])
    pos_s = past + jnp.arange(x_sample.shape[1])
    y_p, y_s = x_prompt, x_sample
    kp_l, vp_l, ikp_l, sp_l, ks_l, vs_l, iks_l, ss_l = [], [], [], [], [], [], [], []
    for l in range(DEPTH):
        lw = (norm_mix[l], w_in[l], gla_a2[l], gla_a_bias[l], q_norm[l], k_norm[l], gla_norm[l],
              w_br_gla[l], w_br_att[l], w_out[l], norm_ffn[l], peer_wq[l], peer_keys[l], peer_u[l], peer_v[l])
        s0_p = jnp.zeros((B, GLA_HEADS, GLA_DK, GLA_DV), x_prompt.dtype)
        y_p, kp, vp, ikp, sp = trunk_layer(y_p, pos_p, s0_p, dsa_prompt, *lw)
        attend_s = functools.partial(dsa_sample, cache_k=cache_k, cache_v=cache_v, cache_idx_k=cache_idx_k,
                                     page_table=page_table, layer=l)
        y_s, kn, vn, ikn, sn = trunk_layer(y_s, pos_s, state_gla[l], attend_s, *lw)
        kp_l.append(kp); vp_l.append(vp); ikp_l.append(ikp); sp_l.append(sp)
        ks_l.append(kn); vs_l.append(vn); iks_l.append(ikn); ss_l.append(sn)
    return (y_p, y_s, jnp.stack(kp_l), jnp.stack(vp_l), jnp.stack(ikp_l), jnp.stack(sp_l),
            jnp.stack(ks_l), jnp.stack(vs_l), jnp.stack(iks_l), jnp.stack(ss_l))
```

```python
import functools
import math

import jax
import jax.numpy as jnp
from jax import lax
from jax.experimental import pallas as pl
from jax.experimental.pallas import tpu as pltpu

D_MODEL = 1024
GLA_HEADS, GLA_DK, GLA_DV = 4, 128, 256
GLA_GATE_RANK = 16
GLA_TAU = 16.0
GLA_CHUNK = 64
ATT_HEADS, ATT_DH = 8, 128
IDX_HEADS, IDX_DH = 8, 64
TOPK_MAX = 256
PAGE_SIZE = 128
ROPE_THETA = 10000.0
PEER_HEADS, PEER_NKEYS, PEER_DKEY, PEER_TOPK = 8, 128, 128, 16
EPS = 1e-6

GLA_QK_W = GLA_HEADS * GLA_DK
GLA_V_W = GLA_HEADS * GLA_DV
ATT_W = ATT_HEADS * ATT_DH
IDX_Q_W = IDX_HEADS * IDX_DH

LANES = 128
VMEM_LIMIT = 56 * 1024 * 1024
MXU_DTYPE = jnp.bfloat16
NEG = -1e30
INT_MIN = -(2 ** 31)

Z_GQK, Z_GV, Z_GR, Z_AQ, Z_AK, Z_AV, Z_GA, Z_GB = (i * 1024 for i in range(8))
Z_IQ = 8 * 1024
Z_MISC = Z_IQ + IDX_Q_W
Z_W = Z_MISC + LANES
MISC_IK, MISC_IW, MISC_GLR = 0, IDX_DH, IDX_DH + IDX_HEADS


def _cparams(sem):
    return pltpu.CompilerParams(dimension_semantics=sem, vmem_limit_bytes=VMEM_LIMIT)


def _row_tile(n, pref):
    t = min(pref, n)
    while n % t:
        t //= 2
    return t


def _proj_kernel(x_ref, g_ref, w_ref, z_ref, h_scr):
    @pl.when(pl.program_id(1) == 0)
    def _():
        x = x_ref[...]
        ms = jnp.mean(x * x, axis=-1, keepdims=True)
        h_scr[...] = (x * lax.rsqrt(ms + EPS) * g_ref[...]).astype(h_scr.dtype)

    z_ref[...] = jnp.dot(h_scr[...], w_ref[...], preferred_element_type=jnp.float32)


def _proj(x, g, w):
    n = x.shape[0]
    tm = _row_tile(n, 512)
    nblk = 3
    tn = Z_W // nblk
    return pl.pallas_call(
        _proj_kernel,
        grid=(n // tm, nblk),
        in_specs=[pl.BlockSpec((tm, D_MODEL), lambda i, j: (i, 0)),
                  pl.BlockSpec((1, D_MODEL), lambda i, j: (0, 0)),
                  pl.BlockSpec((D_MODEL, tn), lambda i, j: (0, j))],
        out_specs=pl.BlockSpec((tm, tn), lambda i, j: (i, j)),
        out_shape=jax.ShapeDtypeStruct((n, Z_W), jnp.float32),
        scratch_shapes=[pltpu.VMEM((tm, D_MODEL), MXU_DTYPE)],
        compiler_params=_cparams(("parallel", "arbitrary")),
        name="proj",
    )(x, g, w)


def _post_kernel(aq_ref, ak_ref, av_ref, iq_ref, misc_ref, c128_ref, s128_ref, c64_ref, s64_ref,
                 qn_ref, kn_ref, q_out, k_out, kb_out, vb_out, iqb_out, misc_out, ikb_out):
    c128, s128 = c128_ref[...], s128_ref[...]

    def norm_rope(x, g):
        ms = jnp.mean(x * x, axis=-1, keepdims=True)
        y = x * lax.rsqrt(ms + EPS) * g
        return y * c128 + pltpu.roll(y, ATT_DH // 2, 1) * s128

    for h in range(ATT_HEADS):
        sl = slice(h * ATT_DH, (h + 1) * ATT_DH)
        q_out[:, sl] = norm_rope(aq_ref[:, sl], qn_ref[...])
        kk = norm_rope(ak_ref[:, sl], kn_ref[...])
        k_out[:, sl] = kk
        kb_out[:, sl] = kk.astype(kb_out.dtype)
    vb_out[...] = av_ref[...].astype(vb_out.dtype)

    half = IDX_DH // 2
    c64, s64 = c64_ref[...], s64_ref[...]

    def rope64(x, reps):
        w = x.shape[1]
        lane = lax.broadcasted_iota(jnp.int32, x.shape, 1)
        first = (lane % IDX_DH) < half
        partner = jnp.where(first, pltpu.roll(x, w - half, 1), pltpu.roll(x, half, 1))
        c = jnp.concatenate([c64] * reps, axis=1) if reps > 1 else c64
        s = jnp.concatenate([s64] * reps, axis=1) if reps > 1 else s64
        return x * c + partner * s

    iq = rope64(iq_ref[...], IDX_Q_W // LANES) * (IDX_DH ** -0.5)
    iqb_out[...] = iq.astype(iqb_out.dtype)
    misc = misc_ref[...]
    rot = rope64(misc, 1)
    lane = lax.broadcasted_iota(jnp.int32, misc.shape, 1)
    out = jnp.where(lane < MISC_IW, rot,
                    jnp.where(lane < MISC_GLR, misc * (IDX_HEADS ** -0.5), misc))
    misc_out[...] = out
    ikb_out[...] = rot[:, :IDX_DH].astype(ikb_out.dtype)


def _post(z, tabs, tab_rows, q_norm, k_norm):
    n = z.shape[0]
    tm = _row_tile(min(n, tab_rows), 512)
    ntab = tab_rows // tm
    c128, s128, c64, s64 = tabs
    zb = lambda off, w: pl.BlockSpec((tm, w), lambda i: (i, off // w))
    tb = pl.BlockSpec((tm, LANES), lambda i: (i % ntab, 0))
    gb = pl.BlockSpec((1, ATT_DH), lambda i: (0, 0))
    ob = lambda w: pl.BlockSpec((tm, w), lambda i: (i, 0))
    f32, bf = jnp.float32, MXU_DTYPE
    return pl.pallas_call(
        _post_kernel,
        grid=(n // tm,),
        in_specs=[zb(Z_AQ, ATT_W), zb(Z_AK, ATT_W), zb(Z_AV, ATT_W), zb(Z_IQ, IDX_Q_W), zb(Z_MISC, LANES),
                  tb, tb, tb, tb, gb, gb],
        out_specs=[ob(ATT_W), ob(ATT_W), ob(ATT_W), ob(ATT_W), ob(IDX_Q_W), ob(LANES), ob(IDX_DH)],
        out_shape=[jax.ShapeDtypeStruct((n, ATT_W), f32), jax.ShapeDtypeStruct((n, ATT_W), f32),
                   jax.ShapeDtypeStruct((n, ATT_W), bf), jax.ShapeDtypeStruct((n, ATT_W), bf),
                   jax.ShapeDtypeStruct((n, IDX_Q_W), bf), jax.ShapeDtypeStruct((n, LANES), f32),
                   jax.ShapeDtypeStruct((n, IDX_DH), bf)],
        compiler_params=_cparams(("parallel",)),
        name="post",
    )(z, z, z, z, z, c128, s128, c64, s64, q_norm, k_norm)


def _rope_tables(pos):
    def tab(dh):
        half = dh // 2
        inv = ROPE_THETA ** (-jnp.arange(half, dtype=jnp.float32) / half)
        ang = pos.astype(jnp.float32)[:, None] * inv[None, :]
        c, s = jnp.cos(ang), jnp.sin(ang)
        reps = LANES // dh
        return (jnp.concatenate([c, c] * reps, axis=1), jnp.concatenate([-s, s] * reps, axis=1))
    c128, s128 = tab(ATT_DH)
    c64, s64 = tab(IDX_DH)
    return c128, s128, c64, s64


def _log_sigmoid(x):
    return jnp.minimum(x, 0.0) - jnp.log(1.0 + jnp.exp(-jnp.abs(x)))


def _gla_prompt_kernel(qk_ref, v_ref, gr_ref, misc_ref, a2_ref, ab_ref, gn_ref, o_ref, sfin_ref, s_scr, *, rows):
    t = pl.program_id(1)
    C = GLA_CHUNK
    hi = lax.Precision.HIGHEST

    @pl.when(t == 0)
    def _():
        s_scr[...] = jnp.zeros_like(s_scr)

    ri = lax.broadcasted_iota(jnp.int32, (C, C), 0)
    ci = lax.broadcasted_iota(jnp.int32, (C, C), 1)
    causal = ri >= ci
    tri = causal.astype(jnp.float32)
    ones = jnp.ones((C, GLA_DK), jnp.float32)
    gn = gn_ref[...]

    def chunk(c, carry):
        r0 = pl.multiple_of(c * C, C)
        rs = pl.ds(r0, C)
        glr = misc_ref[rs, MISC_GLR:MISC_GLR + GLA_GATE_RANK]
        for h in range(GLA_HEADS):
            ks = slice(h * GLA_DK, (h + 1) * GLA_DK)
            vs = slice(h * GLA_DV, (h + 1) * GLA_DV)
            q = qk_ref[rs, ks] * (GLA_DK ** -0.5)
            k = qk_ref[rs, GLA_QK_W + h * GLA_DK:GLA_QK_W + (h + 1) * GLA_DK]
            v = v_ref[rs, vs]
            pre = jnp.dot(glr, a2_ref[:, ks], precision=hi, preferred_element_type=jnp.float32) + ab_ref[:, ks]
            la = _log_sigmoid(pre) / GLA_TAU
            b = jnp.dot(tri, la, precision=hi, preferred_element_type=jnp.float32)
            qd = (q * jnp.exp(b)).astype(MXU_DTYPE)
            kd = (k * jnp.exp(-b)).astype(MXU_DTYPE)
            vb = v.astype(MXU_DTYPE)
            a = lax.dot_general(qd, kd, (((1,), (1,)), ((), ())), preferred_element_type=jnp.float32)
            a = jnp.where(causal, a, 0.0)
            s_old = s_scr[h]
            o = (jnp.dot(a.astype(MXU_DTYPE), vb, preferred_element_type=jnp.float32)
                 + jnp.dot(qd, s_old.astype(MXU_DTYPE), preferred_element_type=jnp.float32))
            b_last = b[C - 1:C, :]
            kl_t = (k * jnp.exp(b_last - b)).T.astype(MXU_DTYPE)
            bl_t = jnp.dot(la.T, ones, precision=hi, preferred_element_type=jnp.float32)
            dec = jnp.exp(bl_t)
            s_scr[h] = (jnp.concatenate([dec] * (GLA_DV // LANES), axis=1) * s_old
                        + jnp.dot(kl_t, vb, preferred_element_type=jnp.float32))
            ms = jnp.mean(o * o, axis=-1, keepdims=True)
            on = o * lax.rsqrt(ms + EPS) * gn
            g = gr_ref[rs, vs]
            o_ref[rs, vs] = (on * (g * jax.nn.sigmoid(g))).astype(o_ref.dtype)
        return carry

    lax.fori_loop(0, rows // C, chunk, 0)

    @pl.when(t == pl.num_programs(1) - 1)
    def _():
        sfin_ref[0] = s_scr[...]


def _gla_prompt(z, a2, ab, gn, batch, seq):
    rows = _row_tile(seq, 512)
    nt = seq // rows
    zb = lambda off, w: pl.BlockSpec((rows, w), lambda b, t: (b * nt + t, off // w))
    full = lambda a: pl.BlockSpec(a.shape, lambda b, t: (0,) * a.ndim)
    return pl.pallas_call(
        functools.partial(_gla_prompt_kernel, rows=rows),
        grid=(batch, nt),
        in_specs=[zb(Z_GQK, 2 * GLA_QK_W), zb(Z_GV, GLA_V_W), zb(Z_GR, GLA_V_W), zb(Z_MISC, LANES),
                  full(a2), full(ab), full(gn)],
        out_specs=[pl.BlockSpec((rows, GLA_V_W), lambda b, t: (b * nt + t, 0)),
                   pl.BlockSpec((1, GLA_HEADS, GLA_DK, GLA_DV), lambda b, t: (b, 0, 0, 0))],
        out_shape=[jax.ShapeDtypeStruct((batch * seq, GLA_V_W), MXU_DTYPE),
                   jax.ShapeDtypeStruct((batch, GLA_HEADS, GLA_DK, GLA_DV), jnp.float32)],
        scratch_shapes=[pltpu.VMEM((GLA_HEADS, GLA_DK, GLA_DV), jnp.float32)],
        compiler_params=_cparams(("parallel", "arbitrary")),
        name="gla_prompt",
    )(z, z, z, z, a2, ab, gn)


def _gla_sample_kernel(qk_ref, v_ref, gr_ref, misc_ref, a2_ref, ab_ref, gn_ref, s0_ref, o_ref, s1_ref):
    hi = lax.Precision.HIGHEST
    eye = (lax.broadcasted_iota(jnp.int32, (GLA_DK, GLA_DK), 0)
           == lax.broadcasted_iota(jnp.int32, (GLA_DK, GLA_DK), 1))

    def col(row):
        return jnp.sum(jnp.where(eye, row, 0.0), axis=1, keepdims=True)

    glr = misc_ref[0][:, MISC_GLR:MISC_GLR + GLA_GATE_RANK]
    qk, v, gr = qk_ref[0], v_ref[0], gr_ref[0]
    for h in range(GLA_HEADS):
        ks = slice(h * GLA_DK, (h + 1) * GLA_DK)
        vs = slice(h * GLA_DV, (h + 1) * GLA_DV)
        q = qk[:, ks] * (GLA_DK ** -0.5)
        k = qk[:, GLA_QK_W + h * GLA_DK:GLA_QK_W + (h + 1) * GLA_DK]
        pre = jnp.dot(glr, a2_ref[:, ks], precision=hi, preferred_element_type=jnp.float32) + ab_ref[:, ks]
        b = _log_sigmoid(pre) / GLA_TAU
        s1 = col(jnp.exp(b)) * s0_ref[0, h] + col(k) * v[:, vs]
        s1_ref[0, h] = s1
        o = jnp.sum(col(q) * s1, axis=0, keepdims=True)
        ms = jnp.mean(o * o, axis=-1, keepdims=True)
        on = o * lax.rsqrt(ms + EPS) * gn_ref[...]
        g = gr[:, vs]
        o_ref[0, :, vs] = (on * (g * jax.nn.sigmoid(g))).astype(o_ref.dtype)


def _gla_sample(z, a2, ab, gn, s0):
    n = z.shape[0]
    z3 = z.reshape(n, 1, Z_W)
    zb = lambda off, w: pl.BlockSpec((1, 1, w), lambda b: (b, 0, off // w))
    full = lambda a: pl.BlockSpec(a.shape, lambda b: (0,) * a.ndim)
    sb = pl.BlockSpec((1, GLA_HEADS, GLA_DK, GLA_DV), lambda b: (b, 0, 0, 0))
    o, s1 = pl.pallas_call(
        _gla_sample_kernel,
        grid=(n,),
        in_specs=[zb(Z_GQK, 2 * GLA_QK_W), zb(Z_GV, GLA_V_W), zb(Z_GR, GLA_V_W), zb(Z_MISC, LANES),
                  full(a2), full(ab), full(gn), sb],
        out_specs=[pl.BlockSpec((1, 1, GLA_V_W), lambda b: (b, 0, 0)), sb],
        out_shape=[jax.ShapeDtypeStruct((n, 1, GLA_V_W), MXU_DTYPE),
                   jax.ShapeDtypeStruct(s0.shape, jnp.float32)],
        compiler_params=_cparams(("parallel",)),
        name="gla_sample",
    )(z3, z3, z3, z3, a2, ab, gn, s0)
    return o.reshape(n, GLA_V_W), s1


def _sortable(x):
    x = jnp.where(x == 0.0, 0.0, x)
    bits = pltpu.bitcast(x, jnp.int32)
    return jnp.where(bits < 0, bits ^ jnp.int32(0x7FFFFFFF), bits)


def _idx_select_kernel(iq_ref, misc_ref, ik_ref, mask_ref, key_scr, j_scr, *, tq, ck, seq, topk):
    qi = pl.program_id(1)
    nck = ((qi + 1) * tq + ck - 1) // ck
    nsub = ck // LANES
    row = qi * tq + lax.broadcasted_iota(jnp.int32, (tq, 1), 0)
    w = misc_ref[:, MISC_IW:MISC_IW + IDX_HEADS]
    iq = iq_ref[...]
    iqh = [iq[:, h * IDX_DH:(h + 1) * IDX_DH] for h in range(IDX_HEADS)]
    wh = [w[:, h:h + 1] for h in range(IDX_HEADS)]

    def score_chunk(c, carry):
        c0 = pl.multiple_of(c * ck, ck)
        ks = ik_ref[pl.ds(c0, ck), :]
        acc = jnp.zeros((tq, ck), jnp.float32)
        for h in range(IDX_HEADS):
            s = lax.dot_general(iqh[h], ks, (((1,), (1,)), ((), ())), preferred_element_type=jnp.float32)
            acc = acc + wh[h] * jnp.maximum(s, 0.0)
        col = c0 + lax.broadcasted_iota(jnp.int32, (1, ck), 1)
        key_scr[:, pl.ds(c0, ck)] = jnp.where(col <= row, _sortable(acc), INT_MIN)
        return carry

    lax.fori_loop(0, nck, score_chunk, 0)

    def count(pred):
        def body(c, acc):
            c0 = pl.multiple_of(c * ck, ck)
            blk = key_scr[:, pl.ds(c0, ck)]
            col = c0 + lax.broadcasted_iota(jnp.int32, (1, ck), 1)
            m = pred(blk, col)
            for u in range(nsub):
                acc = acc + m[:, u * LANES:(u + 1) * LANES]
            return acc
        acc = lax.fori_loop(0, nck, body, jnp.zeros((tq, LANES), jnp.int32))
        return jnp.sum(acc, axis=1, keepdims=True)

    def bit_step(i, ans):
        cand = ans ^ lax.shift_left(jnp.int32(1), jnp.int32(31) - i)
        cnt = count(lambda blk, col: jnp.where(blk >= cand, 1, 0))
        return jnp.where(cnt >= topk, cand, ans)

    ans = lax.fori_loop(0, 32, bit_step, jnp.full((tq, 1), INT_MIN, jnp.int32))
    thr = jnp.maximum(ans, INT_MIN + 1)

    c_gt = count(lambda blk, col: jnp.where(blk > thr, 1, 0))
    c_eq = count(lambda blk, col: jnp.where(blk == thr, 1, 0))
    need = topk - c_gt
    excess = c_eq > need
    j_scr[...] = jnp.full(j_scr.shape, seq, jnp.int32)

    @pl.when(jnp.max(jnp.where(excess, 1, 0)) > 0)
    def _():
        nbits = max(1, (seq - 1).bit_length())

        def jbit(i, jv):
            cand = jv | lax.shift_left(jnp.int32(1), jnp.int32(nbits - 1) - i)
            g = count(lambda blk, col: jnp.where(blk == thr, jnp.where(col < cand, 1, 0), 0))
            return jnp.where(g < need, cand, jv)

        jv = lax.fori_loop(0, nbits, jbit, jnp.zeros((tq, 1), jnp.int32))
        j_scr[...] = jnp.broadcast_to(jnp.where(excess, jv, seq), j_scr.shape)

    jlim = j_scr[:, 0:1]

    def write_chunk(c, carry):
        c0 = pl.multiple_of(c * ck, ck)
        blk = key_scr[:, pl.ds(c0, ck)]
        col = c0 + lax.broadcasted_iota(jnp.int32, (1, ck), 1)
        sel = jnp.where(blk > thr, 1, jnp.where(blk == thr, jnp.where(col <= jlim, 1, 0), 0))
        sel = jnp.where(c < nck, sel, 0)
        mask_ref[:, pl.ds(c0, ck)] = sel.astype(mask_ref.dtype)
        return carry

    lax.fori_loop(0, seq // ck, write_chunk, 0)


def _idx_select(iqb, misc2, ikb, batch, seq, topk):
    tq = _row_tile(seq, 256)
    ck = _row_tile(seq, 512)
    nq = seq // tq
    return pl.pallas_call(
        functools.partial(_idx_select_kernel, tq=tq, ck=ck, seq=seq, topk=topk),
        grid=(batch, nq),
        in_specs=[pl.BlockSpec((tq, IDX_Q_W), lambda b, q: (b * nq + q, 0)),
                  pl.BlockSpec((tq, LANES), lambda b, q: (b * nq + q, 0)),
                  pl.BlockSpec((seq, IDX_DH), lambda b, q: (b, 0))],
        out_specs=pl.BlockSpec((tq, seq), lambda b, q: (b * nq + q, 0)),
        out_shape=jax.ShapeDtypeStruct((batch * seq, seq), jnp.int8),
        scratch_shapes=[pltpu.VMEM((tq, seq), jnp.int32), pltpu.VMEM((tq, LANES), jnp.int32)],
        compiler_params=_cparams(("parallel", "arbitrary")),
        name="idx_select",
    )(iqb, misc2, ikb)


def _attn_kernel(q_ref, k_ref, v_ref, m_ref, o_ref, qb_scr, acc_scr, m_scr, l_scr):
    qi, ki = pl.program_id(1), pl.program_id(2)

    @pl.when(ki == 0)
    def _():
        qb_scr[...] = q_ref[...].astype(qb_scr.dtype)
        acc_scr[...] = jnp.zeros_like(acc_scr)
        m_scr[...] = jnp.full(m_scr.shape, NEG, jnp.float32)
        l_scr[...] = jnp.zeros_like(l_scr)

    @pl.when(ki <= qi)
    def _():
        bias = jnp.where(m_ref[...].astype(jnp.int32) != 0, 0.0, NEG)
        for h in range(ATT_HEADS):
            sl = slice(h * ATT_DH, (h + 1) * ATT_DH)
            s = lax.dot_general(qb_scr[:, sl], k_ref[:, sl], (((1,), (1,)), ((), ())),
                                preferred_element_type=jnp.float32)
            s = s * (ATT_DH ** -0.5) + bias
            m_prev = m_scr[h][:, 0:1]
            m_new = jnp.maximum(m_prev, jnp.max(s, axis=1, keepdims=True))
            alpha = jnp.exp(m_prev - m_new)
            p = jnp.exp(s - m_new)
            l_scr[h] = jnp.broadcast_to(alpha * l_scr[h][:, 0:1] + jnp.sum(p, axis=1, keepdims=True), l_scr.shape[1:])
            m_scr[h] = jnp.broadcast_to(m_new, m_scr.shape[1:])
            acc_scr[:, sl] = alpha * acc_scr[:, sl] + jnp.dot(p.astype(v_ref.dtype), v_ref[:, sl],
                                                               preferred_element_type=jnp.float32)

    @pl.when(ki == qi)
    def _():
        for h in range(ATT_HEADS):
            sl = slice(h * ATT_DH, (h + 1) * ATT_DH)
            o_ref[:, sl] = (acc_scr[:, sl] / l_scr[h][:, 0:1]).astype(o_ref.dtype)


def _attn(q, kb, vb, mask, batch, seq):
    t = _row_tile(seq, 512)
    nt = seq // t
    return pl.pallas_call(
        _attn_kernel,
        grid=(batch, nt, nt),
        in_specs=[pl.BlockSpec((t, ATT_W), lambda b, i, j: (b * nt + i, 0)),
                  pl.BlockSpec((t, ATT_W), lambda b, i, j: (b * nt + jnp.minimum(i, j), 0)),
                  pl.BlockSpec((t, ATT_W), lambda b, i, j: (b * nt + jnp.minimum(i, j), 0)),
                  pl.BlockSpec((t, t), lambda b, i, j: (b * nt + i, jnp.minimum(i, j)))],
        out_specs=pl.BlockSpec((t, ATT_W), lambda b, i, j: (b * nt + i, 0)),
        out_shape=jax.ShapeDtypeStruct((batch * seq, ATT_W), MXU_DTYPE),
        scratch_shapes=[pltpu.VMEM((t, ATT_W), MXU_DTYPE), pltpu.VMEM((t, ATT_W), jnp.float32),
                        pltpu.VMEM((ATT_HEADS, t, LANES), jnp.float32),
                        pltpu.VMEM((ATT_HEADS, t, LANES), jnp.float32)],
        compiler_params=_cparams(("parallel", "parallel", "arbitrary")),
        name="attn",
    )(q, kb, vb, mask)


def _sidx_kernel(pt_ref, iq_ref, w_ref, ikn_ref, cache_ref, out_ref, buf, sem, *, n_pages):
    b = pl.program_id(0)
    nb = pl.num_programs(0)
    past = n_pages * PAGE_SIZE

    def page_copy(bb, slot, p):
        return pltpu.make_async_copy(cache_ref.at[pt_ref[bb, p]],
                                     buf.at[slot, pl.ds(p * PAGE_SIZE, PAGE_SIZE)], sem.at[slot])

    def start(bb, slot):
        for p in range(n_pages):
            page_copy(bb, slot, p).start()

    @pl.when(b == 0)
    def _():
        start(0, 0)

    @pl.when(b + 1 < nb)
    def _():
        start(b + 1, (b + 1) % 2)

    slot = b % 2
    for p in range(n_pages):
        page_copy(b, slot, p).wait()

    iq = iq_ref[0]
    w = w_ref[0]
    ik = buf[slot].astype(iq.dtype)
    s = lax.dot_general(iq, ik, (((1,), (1,)), ((), ())), preferred_element_type=jnp.float32)
    sc = jnp.sum(w * jnp.maximum(s, 0.0), axis=0, keepdims=True)
    out_ref[0, :, 0:past] = jnp.where(sc == 0.0, 0.0, sc)
    s_new = jnp.sum(iq.astype(jnp.float32) * ikn_ref[0].astype(iq.dtype).astype(jnp.float32), axis=1, keepdims=True)
    sc_new = jnp.sum(w * jnp.maximum(s_new, 0.0), axis=0, keepdims=True)
    sc_new = jnp.where(sc_new == 0.0, 0.0, sc_new)
    lane = lax.broadcasted_iota(jnp.int32, (1, LANES), 1)
    out_ref[0, :, past:past + LANES] = jnp.where(lane == 0, sc_new, -jnp.inf)


def _sidx(page_table, iq3, w3, ikn3, cache_idx):
    bd, n_pages = page_table.shape
    past = n_pages * PAGE_SIZE
    grid_spec = pltpu.PrefetchScalarGridSpec(
        num_scalar_prefetch=1,
        grid=(bd,),
        in_specs=[pl.BlockSpec((1, IDX_HEADS, IDX_DH), lambda b, pt: (b, 0, 0)),
                  pl.BlockSpec((1, IDX_HEADS, 1), lambda b, pt: (b, 0, 0)),
                  pl.BlockSpec((1, 1, IDX_DH), lambda b, pt: (b, 0, 0)),
                  pl.BlockSpec(memory_space=pl.ANY)],
        out_specs=pl.BlockSpec((1, 1, past + LANES), lambda b, pt: (b, 0, 0)),
        scratch_shapes=[pltpu.VMEM((2, past, IDX_DH), jnp.float32), pltpu.SemaphoreType.DMA((2,))],
    )
    return pl.pallas_call(
        functools.partial(_sidx_kernel, n_pages=n_pages),
        grid_spec=grid_spec,
        out_shape=jax.ShapeDtypeStruct((bd, 1, past + LANES), jnp.float32),
        compiler_params=_cparams(("arbitrary",)),
        name="sidx",
    )(page_table, iq3, w3, ikn3, cache_idx)


def _stopk_kernel(sc_ref, sel_ref, x_scr, *, topk):
    x_scr[...] = sc_ref[...]
    shape = x_scr.shape
    big = jnp.int32(2 ** 30)

    def body(it, sel):
        x = x_scr[...]
        col = lax.broadcasted_iota(jnp.int32, shape, 1)
        m = jnp.max(x, axis=1, keepdims=True)
        idx = jnp.min(jnp.where(x == m, col, big), axis=1, keepdims=True)
        x_scr[...] = jnp.where(col == idx, -jnp.inf, x)
        kcol = lax.broadcasted_iota(jnp.int32, sel.shape, 1)
        return jnp.where(kcol == it, idx, sel)

    sel_ref[...] = lax.fori_loop(0, topk, body, jnp.zeros(sel_ref.shape, jnp.int32))


def _stopk(scores, topk):
    bd, lp = scores.shape
    return pl.pallas_call(
        functools.partial(_stopk_kernel, topk=topk),
        out_shape=jax.ShapeDtypeStruct((bd, topk), jnp.int32),
        scratch_shapes=[pltpu.VMEM((bd, lp), jnp.float32)],
        compiler_params=pltpu.CompilerParams(vmem_limit_bytes=VMEM_LIMIT),
        name="stopk",
    )(scores)


def _sattn_kernel(sel_ref, pt_ref, q_ref, kn_ref, vn_ref, selv_ref, ck_ref, cv_ref, o_ref, kbuf, vbuf, sem,
                  *, topk, past):
    b = pl.program_id(0)
    nb = pl.num_programs(0)

    def row_copies(bb, slot, j):
        p = jnp.minimum(sel_ref[bb, j], past - 1)
        phys = pt_ref[bb, p // PAGE_SIZE]
        r = p % PAGE_SIZE
        return (pltpu.make_async_copy(ck_ref.at[phys, r], kbuf.at[slot, j], sem.at[0, slot]),
                pltpu.make_async_copy(cv_ref.at[phys, r], vbuf.at[slot, j], sem.at[1, slot]))

    def start(bb, slot):
        def body(j, c):
            ck, cv = row_copies(bb, slot, j)
            ck.start()
            cv.start()
            return c
        lax.fori_loop(0, topk, body, 0)

    @pl.when(b == 0)
    def _():
        start(0, 0)

    @pl.when(b + 1 < nb)
    def _():
        start(b + 1, (b + 1) % 2)

    slot = b % 2

    def wbody(j, c):
        ck, cv = row_copies(b, slot, j)
        ck.wait()
        cv.wait()
        return c

    lax.fori_loop(0, topk, wbody, 0)

    selv = selv_ref[0]
    in_past = selv < past
    q = q_ref[...]
    kg = jnp.where(in_past, kbuf[slot], kn_ref[...])
    vg = jnp.where(in_past, vbuf[slot], vn_ref[...])
    s = jnp.sum(kg * q, axis=-1, keepdims=True) * (ATT_DH ** -0.5)
    s = jnp.where(selv <= past, s, -jnp.inf)
    m = jnp.max(s, axis=0, keepdims=True)
    p = jnp.exp(s - m)
    p = p / jnp.sum(p, axis=0, keepdims=True)
    o_ref[...] = jnp.sum(p * vg, axis=0, keepdims=True).astype(o_ref.dtype)


def _sattn(sel, page_table, q3, kn3, vn3, cache_k, cache_v):
    bd, topk = sel.shape
    past = page_table.shape[1] * PAGE_SIZE
    hb = pl.BlockSpec((1, ATT_HEADS, ATT_DH), lambda b, s, pt: (b, 0, 0))
    grid_spec = pltpu.PrefetchScalarGridSpec(
        num_scalar_prefetch=2,
        grid=(bd,),
        in_specs=[hb, hb, hb, pl.BlockSpec((1, topk, 1, 1), lambda b, s, pt: (b, 0, 0, 0)),
                  pl.BlockSpec(memory_space=pl.ANY), pl.BlockSpec(memory_space=pl.ANY)],
        out_specs=hb,
        scratch_shapes=[pltpu.VMEM((2, topk, ATT_HEADS, ATT_DH), jnp.float32),
                        pltpu.VMEM((2, topk, ATT_HEADS, ATT_DH), jnp.float32),
                        pltpu.SemaphoreType.DMA((2, 2))],
    )
    return pl.pallas_call(
        functools.partial(_sattn_kernel, topk=topk, past=past),
        grid_spec=grid_spec,
        out_shape=jax.ShapeDtypeStruct((bd, ATT_HEADS, ATT_DH), MXU_DTYPE),
        compiler_params=_cparams(("arbitrary",)),
        name="sattn",
    )(sel, page_table, q3, kn3, vn3, sel.reshape(bd, topk, 1, 1), cache_k, cache_v)


def _merge_kernel(og_ref, oa_ref, ga_ref, gb_ref, x_ref, wg_ref, wa_ref, wo_ref, nf_ref, x1_ref, hn_ref):
    f32 = jnp.float32
    mix = (jax.nn.sigmoid(ga_ref[...]) * jnp.dot(og_ref[...], wg_ref[...], preferred_element_type=f32)
           + jax.nn.sigmoid(gb_ref[...]) * jnp.dot(oa_ref[...], wa_ref[...], preferred_element_type=f32))
    x1 = x_ref[...] + jnp.dot(mix.astype(wo_ref.dtype), wo_ref[...], preferred_element_type=f32)
    x1_ref[...] = x1
    ms = jnp.mean(x1 * x1, axis=-1, keepdims=True)
    hn_ref[...] = (x1 * lax.rsqrt(ms + EPS) * nf_ref[...]).astype(hn_ref.dtype)


def _merge(og, oa, z, x, wg, wa, wo, nf):
    n = x.shape[0]
    tm = _row_tile(n, 512)
    rb = lambda: pl.BlockSpec((tm, D_MODEL), lambda i: (i, 0))
    zb = lambda off: pl.BlockSpec((tm, D_MODEL), lambda i: (i, off // D_MODEL))
    wb = lambda: pl.BlockSpec((D_MODEL, D_MODEL), lambda i: (0, 0))
    return pl.pallas_call(
        _merge_kernel,
        grid=(n // tm,),
        in_specs=[rb(), rb(), zb(Z_GA), zb(Z_GB), rb(), wb(), wb(), wb(), pl.BlockSpec((1, D_MODEL), lambda i: (0, 0))],
        out_specs=[rb(), rb()],
        out_shape=[jax.ShapeDtypeStruct((n, D_MODEL), jnp.float32), jax.ShapeDtypeStruct((n, D_MODEL), MXU_DTYPE)],
        compiler_params=_cparams(("parallel",)),
        name="merge",
    )(og, oa, z, z, x, wg, wa, wo, nf)


_PAIRS = [(r1, r2) for r1 in range(PEER_TOPK) for r2 in range(PEER_TOPK) if (r1 + 1) * (r2 + 1) <= PEER_TOPK]


def _top16_sorted(x):
    nk = x.shape[0]
    rid = lax.broadcasted_iota(jnp.int32, x.shape, 0)
    rank = jnp.full(x.shape, PEER_TOPK, jnp.int32)
    vals = []
    for r in range(PEER_TOPK):
        m = jnp.max(x, axis=0, keepdims=True)
        idx = jnp.min(jnp.where(x == m, rid, nk), axis=0, keepdims=True)
        hit = rid == idx
        x = jnp.where(hit, -jnp.inf, x)
        rank = jnp.where(hit, r, rank)
        vals.append(m)
    return vals, rank


def _router_kernel(hn_ref, wqt_ref, keys_ref, s2m_ref, e2_ref, th_ref, e1_ref):
    f32 = jnp.float32
    qt = lax.dot_general(wqt_ref[...], hn_ref[...], (((1,), (1,)), ((), ())), preferred_element_type=f32)
    qt = qt.astype(MXU_DTYPE)
    half = PEER_DKEY // 2
    npair = len(_PAIRS)
    for h in range(PEER_HEADS):
        s1 = jnp.dot(keys_ref[2 * h], qt[(2 * h) * half:(2 * h + 1) * half, :], preferred_element_type=f32)
        s2 = jnp.dot(keys_ref[2 * h + 1], qt[(2 * h + 1) * half:(2 * h + 2) * half, :], preferred_element_type=f32)
        v1, rank1 = _top16_sorted(s1)
        v2, rank2 = _top16_sorted(s2)
        cand = jnp.concatenate([v1[a] + v2[b] for a, b in _PAIRS], axis=0)
        pid = lax.broadcasted_iota(jnp.int32, cand.shape, 0)
        c = cand
        picked = jnp.zeros(cand.shape, jnp.int32)
        for _ in range(PEER_TOPK):
            m = jnp.max(c, axis=0, keepdims=True)
            idx = jnp.min(jnp.where(c == m, pid, npair), axis=0, keepdims=True)
            hit = pid == idx
            c = jnp.where(hit, -jnp.inf, c)
            picked = jnp.where(hit, 1, picked)
        top = v1[0] + v2[0]
        zsum = jnp.sum(jnp.where(picked > 0, jnp.exp(cand - top), 0.0), axis=0, keepdims=True)
        inv_z = 1.0 / zsum
        th_by_rank = []
        for a in range(PEER_TOPK):
            rows = [i for i, (pa, _) in enumerate(_PAIRS) if pa == a]
            lo, hi = rows[0], rows[-1] + 1
            v2blk = jnp.concatenate([v2[_PAIRS[i][1]] for i in rows], axis=0)
            th_by_rank.append(jnp.min(jnp.where(picked[lo:hi] > 0, v2blk, jnp.inf), axis=0, keepdims=True))
        th = jnp.full(s1.shape, jnp.inf, f32)
        for a in range(PEER_TOPK):
            th = jnp.where(rank1 == a, th_by_rank[a], th)
        sel1 = rank1 < PEER_TOPK
        sel2 = rank2 < PEER_TOPK
        th_ref[0, h] = th
        e1_ref[0, h] = jnp.where(sel1, jnp.exp(s1 - v1[0]) * inv_z, 0.0)
        s2m_ref[0, h] = jnp.where(sel2, s2, -jnp.inf)
        e2_ref[0, h] = jnp.where(sel2, jnp.exp(s2 - v2[0]), 0.0)


def _router(hn, wqt, keys, tm):
    n = hn.shape[0]
    nt = n // tm
    shp = jax.ShapeDtypeStruct((nt, PEER_HEADS, PEER_NKEYS, tm), jnp.float32)
    ob = pl.BlockSpec((1, PEER_HEADS, PEER_NKEYS, tm), lambda i: (i, 0, 0, 0))
    return pl.pallas_call(
        _router_kernel,
        grid=(nt,),
        in_specs=[pl.BlockSpec((tm, D_MODEL), lambda i: (i, 0)),
                  pl.BlockSpec((D_MODEL, D_MODEL), lambda i: (0, 0)),
                  pl.BlockSpec(keys.shape, lambda i: (0, 0, 0))],
        out_specs=[ob, ob, ob, ob],
        out_shape=[shp, shp, shp, shp],
        compiler_params=_cparams(("parallel",)),
        name="router",
    )(hn, wqt, keys)


def _gelu(x):
    return 0.5 * x * (1.0 + lax.erf(x * (2.0 ** -0.5)))


def _peer_kernel(hn_ref, u_ref, vt_ref, s2m_ref, e2_ref, th_ref, e1_ref, x1_ref, y_ref, acc_scr, p_scr, *, rows1):
    j = pl.program_id(1)
    f32 = jnp.float32

    @pl.when(j == 0)
    def _():
        acc_scr[...] = jnp.zeros_like(acc_scr)

    at = lax.dot_general(u_ref[...], hn_ref[...], (((1,), (1,)), ((), ())), preferred_element_type=f32)
    for c in range(rows1):
        w = jnp.zeros((PEER_NKEYS, at.shape[1]), f32)
        for h in range(PEER_HEADS):
            w = w + jnp.where(s2m_ref[0, h] >= th_ref[0, h, c:c + 1, :], e2_ref[0, h], 0.0) * e1_ref[0, h, c:c + 1, :]
        rs = slice(c * PEER_NKEYS, (c + 1) * PEER_NKEYS)
        p_scr[rs, :] = (w * _gelu(at[rs, :])).astype(p_scr.dtype)
    acc_scr[...] += jnp.dot(vt_ref[...], p_scr[...], preferred_element_type=f32)

    @pl.when(j == pl.num_programs(1) - 1)
    def _():
        y_ref[...] = x1_ref[...] + acc_scr[...].T


def _peer(hn, u, vt, router_out, x1, tm):
    n = hn.shape[0]
    nt = n // tm
    rows1 = 16
    te = rows1 * PEER_NKEYS
    ne = (PEER_NKEYS * PEER_NKEYS) // te
    s2m, e2, th, e1 = router_out
    full = pl.BlockSpec((1, PEER_HEADS, PEER_NKEYS, tm), lambda i, j: (i, 0, 0, 0))
    part = pl.BlockSpec((1, PEER_HEADS, rows1, tm), lambda i, j: (i, 0, j, 0))
    return pl.pallas_call(
        functools.partial(_peer_kernel, rows1=rows1),
        grid=(nt, ne),
        in_specs=[pl.BlockSpec((tm, D_MODEL), lambda i, j: (i, 0)),
                  pl.BlockSpec((te, D_MODEL), lambda i, j: (j, 0)),
                  pl.BlockSpec((D_MODEL, te), lambda i, j: (0, j)),
                  full, full, part, part,
                  pl.BlockSpec((tm, D_MODEL), lambda i, j: (i, 0))],
        out_specs=pl.BlockSpec((tm, D_MODEL), lambda i, j: (i, 0)),
        out_shape=jax.ShapeDtypeStruct((n, D_MODEL), jnp.float32),
        scratch_shapes=[pltpu.VMEM((D_MODEL, tm), jnp.float32), pltpu.VMEM((te, tm), MXU_DTYPE)],
        compiler_params=_cparams(("parallel", "arbitrary")),
        name="peer",
    )(hn, u, vt, s2m, e2, th, e1, x1)


def _reorder_w_in(w_in):
    sizes = (GLA_QK_W, GLA_QK_W, GLA_V_W, GLA_V_W, GLA_GATE_RANK, ATT_W, ATT_W, ATT_W, IDX_Q_W, IDX_DH, IDX_HEADS,
             D_MODEL, D_MODEL)
    offs = [0]
    for s in sizes:
        offs.append(offs[-1] + s)
    gq, gk, gv, gr, glr, aq, ak, av, iq, ik, iw, ga, gb = (w_in[:, offs[i]:offs[i + 1]] for i in range(len(sizes)))
    pad = jnp.zeros((w_in.shape[0], LANES - IDX_DH - IDX_HEADS - GLA_GATE_RANK), w_in.dtype)
    return jnp.concatenate([gq, gk, gv, gr, aq, ak, av, ga, gb, iq, ik, iw, glr, pad], axis=1)


def _ffn(og, oa, z, x, wg, wa, wo, nf, wqt, keys, u, vt):
    n = x.shape[0]
    x1, hn = _merge(og, oa, z, x, wg, wa, wo, nf)
    tm = _row_tile(n, 512)
    r = _router(hn, wqt, keys, tm)
    return _peer(hn, u, vt, r, x1, tm)


def kernel(x_prompt, x_sample, cache_k, cache_v, cache_idx_k, state_gla, page_table, norm_mix, w_in, gla_a2,
           gla_a_bias, q_norm, k_norm, gla_norm, w_br_gla, w_br_att, w_out, norm_ffn, peer_wq, peer_keys,
           peer_u, peer_v):
    depth = w_in.shape[0]
    assert depth == 1 and x_sample.shape[1] == 1
    B, S, _ = x_prompt.shape
    Bd = x_sample.shape[0]
    n_pages = page_table.shape[1]
    past = n_pages * PAGE_SIZE
    bf = MXU_DTYPE
    l = 0

    w_all = _reorder_w_in(w_in[l]).astype(bf)
    g_mix = norm_mix[l].reshape(1, D_MODEL)
    a2, ab = gla_a2[l], gla_a_bias[l].reshape(1, GLA_QK_W)
    gn = gla_norm[l].reshape(1, GLA_DV)
    qn, kn = q_norm[l].reshape(1, ATT_DH), k_norm[l].reshape(1, ATT_DH)
    wg, wa, wo = w_br_gla[l].astype(bf), w_br_att[l].astype(bf), w_out[l].astype(bf)
    nf = norm_ffn[l].reshape(1, D_MODEL)
    wqt = peer_wq[l].T.astype(bf)
    keys = peer_keys[l].reshape(PEER_HEADS * 2, PEER_NKEYS, PEER_DKEY // 2).astype(bf)
    u = peer_u[l].astype(bf)
    vt = peer_v[l].T.astype(bf)
    ffn_w = (wg, wa, wo, nf, wqt, keys, u, vt)

    xp = x_prompt.reshape(B * S, D_MODEL)
    zp = _proj(xp, g_mix, w_all)
    tabs_p = _rope_tables(jnp.arange(S))
    qp, kp, kpb, vpb, iqpb, miscp, ikpb = _post(zp, tabs_p, S, qn, kn)
    ogp, sp = _gla_prompt(zp, a2, ab, gn, B, S)
    topk_p = min(TOPK_MAX, S // 4)
    maskp = _idx_select(iqpb, miscp, ikpb, B, S, topk_p)
    oap = _attn(qp, kpb, vpb, maskp, B, S)
    yp = _ffn(ogp, oap, zp, xp, *ffn_w)

    xs = x_sample.reshape(Bd, D_MODEL)
    zs = _proj(xs, g_mix, w_all)
    tabs_s = _rope_tables(jnp.full((Bd,), past, jnp.int32))
    qs, ks, _, _, iqsb, miscs, iksb = _post(zs, tabs_s, Bd, qn, kn)
    vs = zs[:, Z_AV:Z_AV + ATT_W]
    ogs, ss = _gla_sample(zs, a2, ab, gn, state_gla[l])
    scores = _sidx(page_table, iqsb.reshape(Bd, IDX_HEADS, IDX_DH),
                   miscs[:, MISC_IW:MISC_IW + IDX_HEADS].reshape(Bd, IDX_HEADS, 1),
                   iksb.reshape(Bd, 1, IDX_DH), cache_idx_k[l])
    topk_s = min(TOPK_MAX, (past + 1) // 4)
    sel = _stopk(scores.reshape(Bd, past + LANES), topk_s)
    hd = (Bd, ATT_HEADS, ATT_DH)
    oas = _sattn(sel, page_table, qs.reshape(hd), ks.reshape(hd), vs.reshape(hd), cache_k[l], cache_v[l])
    ys = _ffn(ogs, oas.reshape(Bd, ATT_W), zs, xs, *ffn_w)

    return (yp.reshape(B, S, D_MODEL), ys.reshape(Bd, 1, D_MODEL),
            kp.reshape(1, B, S, ATT_HEADS, ATT_DH), zp[:, Z_AV:Z_AV + ATT_W].reshape(1, B, S, ATT_HEADS, ATT_DH),
            miscp[:, :IDX_DH].reshape(1, B, S, IDX_DH), sp[None],
            ks.reshape(1, Bd, 1, ATT_HEADS, ATT_DH), vs.reshape(1, Bd, 1, ATT_HEADS, ATT_DH),
            miscs[:, :IDX_DH].reshape(1, Bd, 1, IDX_DH), ss[None])
```

```python
import functools
import math

import jax
import jax.numpy as jnp
from jax import lax
from jax.experimental import pallas as pl
from jax.experimental.pallas import tpu as pltpu

D_MODEL = 1024
GLA_HEADS, GLA_DK, GLA_DV = 4, 128, 256
GLA_GATE_RANK = 16
GLA_TAU = 16.0
GLA_CHUNK = 64
ATT_HEADS, ATT_DH = 8, 128
IDX_HEADS, IDX_DH = 8, 64
TOPK_MAX = 256
PAGE_SIZE = 128
ROPE_THETA = 10000.0
PEER_HEADS, PEER_NKEYS, PEER_DKEY, PEER_TOPK = 8, 128, 128, 16
EPS = 1e-6

GLA_QK_W = GLA_HEADS * GLA_DK
GLA_V_W = GLA_HEADS * GLA_DV
ATT_W = ATT_HEADS * ATT_DH
IDX_Q_W = IDX_HEADS * IDX_DH

LANES = 128
VMEM_LIMIT = 56 * 1024 * 1024
MXU_DTYPE = jnp.bfloat16
GATE_DTYPE = jnp.bfloat16
NEG = -1e30
INT_MIN = -(2 ** 31)

Z_GQK, Z_GV, Z_GR, Z_AQ, Z_AK, Z_AV, Z_GA, Z_GB = (i * 1024 for i in range(8))
Z_IQ = 8 * 1024
Z_MISC = Z_IQ + IDX_Q_W
Z_W = Z_MISC + LANES
MISC_IK, MISC_IW, MISC_GLR = 0, IDX_DH, IDX_DH + IDX_HEADS


def _cparams(sem):
    return pltpu.CompilerParams(dimension_semantics=sem, vmem_limit_bytes=VMEM_LIMIT)


def _row_tile(n, pref):
    t = min(pref, n)
    while n % t:
        t //= 2
    return t


def _proj_kernel(x_ref, g_ref, w_ref, z_ref, h_scr):
    @pl.when(pl.program_id(1) == 0)
    def _():
        x = x_ref[...]
        ms = jnp.mean(x * x, axis=-1, keepdims=True)
        h_scr[...] = (x * lax.rsqrt(ms + EPS) * g_ref[...]).astype(h_scr.dtype)

    z_ref[...] = jnp.dot(h_scr[...], w_ref[...], preferred_element_type=jnp.float32)


def _proj(x, g, w):
    n = x.shape[0]
    tm = _row_tile(n, 512)
    nblk = 3
    tn = Z_W // nblk
    return pl.pallas_call(
        _proj_kernel,
        grid=(n // tm, nblk),
        in_specs=[pl.BlockSpec((tm, D_MODEL), lambda i, j: (i, 0)),
                  pl.BlockSpec((1, D_MODEL), lambda i, j: (0, 0)),
                  pl.BlockSpec((D_MODEL, tn), lambda i, j: (0, j))],
        out_specs=pl.BlockSpec((tm, tn), lambda i, j: (i, j)),
        out_shape=jax.ShapeDtypeStruct((n, Z_W), jnp.float32),
        scratch_shapes=[pltpu.VMEM((tm, D_MODEL), MXU_DTYPE)],
        compiler_params=_cparams(("parallel", "arbitrary")),
        name="proj",
    )(x, g, w)


def _post_kernel(aq_ref, ak_ref, av_ref, iq_ref, misc_ref, c128_ref, s128_ref, c64_ref, s64_ref,
                 qn_ref, kn_ref, q_out, k_out, kb_out, vb_out, iqb_out, misc_out, ikb_out):
    c128, s128 = c128_ref[...], s128_ref[...]

    def norm_rope(x, g):
        ms = jnp.mean(x * x, axis=-1, keepdims=True)
        y = x * lax.rsqrt(ms + EPS) * g
        return y * c128 + pltpu.roll(y, ATT_DH // 2, 1) * s128

    for h in range(ATT_HEADS):
        sl = slice(h * ATT_DH, (h + 1) * ATT_DH)
        q_out[:, sl] = norm_rope(aq_ref[:, sl], qn_ref[...])
        kk = norm_rope(ak_ref[:, sl], kn_ref[...])
        k_out[:, sl] = kk
        kb_out[:, sl] = kk.astype(kb_out.dtype)
    vb_out[...] = av_ref[...].astype(vb_out.dtype)

    half = IDX_DH // 2
    c64, s64 = c64_ref[...], s64_ref[...]

    def rope64(x, reps):
        w = x.shape[1]
        lane = lax.broadcasted_iota(jnp.int32, x.shape, 1)
        first = (lane % IDX_DH) < half
        partner = jnp.where(first, pltpu.roll(x, w - half, 1), pltpu.roll(x, half, 1))
        c = jnp.concatenate([c64] * reps, axis=1) if reps > 1 else c64
        s = jnp.concatenate([s64] * reps, axis=1) if reps > 1 else s64
        return x * c + partner * s

    iq = rope64(iq_ref[...], IDX_Q_W // LANES) * (IDX_DH ** -0.5)
    iqb_out[...] = iq.astype(iqb_out.dtype)
    misc = misc_ref[...]
    rot = rope64(misc, 1)
    lane = lax.broadcasted_iota(jnp.int32, misc.shape, 1)
    out = jnp.where(lane < MISC_IW, rot,
                    jnp.where(lane < MISC_GLR, misc * (IDX_HEADS ** -0.5), misc))
    misc_out[...] = out
    ikb_out[...] = rot[:, :IDX_DH].astype(ikb_out.dtype)


def _post(z, tabs, tab_rows, q_norm, k_norm):
    n = z.shape[0]
    tm = _row_tile(min(n, tab_rows), 512)
    ntab = tab_rows // tm
    c128, s128, c64, s64 = tabs
    zb = lambda off, w: pl.BlockSpec((tm, w), lambda i: (i, off // w))
    tb = pl.BlockSpec((tm, LANES), lambda i: (i % ntab, 0))
    gb = pl.BlockSpec((1, ATT_DH), lambda i: (0, 0))
    ob = lambda w: pl.BlockSpec((tm, w), lambda i: (i, 0))
    f32, bf = jnp.float32, MXU_DTYPE
    return pl.pallas_call(
        _post_kernel,
        grid=(n // tm,),
        in_specs=[zb(Z_AQ, ATT_W), zb(Z_AK, ATT_W), zb(Z_AV, ATT_W), zb(Z_IQ, IDX_Q_W), zb(Z_MISC, LANES),
                  tb, tb, tb, tb, gb, gb],
        out_specs=[ob(ATT_W), ob(ATT_W), ob(ATT_W), ob(ATT_W), ob(IDX_Q_W), ob(LANES), ob(IDX_DH)],
        out_shape=[jax.ShapeDtypeStruct((n, ATT_W), f32), jax.ShapeDtypeStruct((n, ATT_W), f32),
                   jax.ShapeDtypeStruct((n, ATT_W), bf), jax.ShapeDtypeStruct((n, ATT_W), bf),
                   jax.ShapeDtypeStruct((n, IDX_Q_W), bf), jax.ShapeDtypeStruct((n, LANES), f32),
                   jax.ShapeDtypeStruct((n, IDX_DH), bf)],
        compiler_params=_cparams(("parallel",)),
        name="post",
    )(z, z, z, z, z, c128, s128, c64, s64, q_norm, k_norm)


def _rope_tables(pos):
    def tab(dh):
        half = dh // 2
        inv = ROPE_THETA ** (-jnp.arange(half, dtype=jnp.float32) / half)
        ang = pos.astype(jnp.float32)[:, None] * inv[None, :]
        c, s = jnp.cos(ang), jnp.sin(ang)
        reps = LANES // dh
        return (jnp.concatenate([c, c] * reps, axis=1), jnp.concatenate([-s, s] * reps, axis=1))
    c128, s128 = tab(ATT_DH)
    c64, s64 = tab(IDX_DH)
    return c128, s128, c64, s64


def _log_sigmoid(x):
    return jnp.minimum(x, 0.0) - jnp.log(1.0 + jnp.exp(-jnp.abs(x)))


def _gla_prompt_kernel(qk_ref, v_ref, gr_ref, misc_ref, a2_ref, ab_ref, gn_ref, o_ref, sfin_ref, s_scr, *, rows):
    t = pl.program_id(1)
    C = GLA_CHUNK
    hi = lax.Precision.HIGHEST

    @pl.when(t == 0)
    def _():
        s_scr[...] = jnp.zeros_like(s_scr)

    ri = lax.broadcasted_iota(jnp.int32, (C, C), 0)
    ci = lax.broadcasted_iota(jnp.int32, (C, C), 1)
    causal = ri >= ci
    tri = causal.astype(jnp.float32)
    ones = jnp.ones((C, GLA_DK), jnp.float32)
    gn = gn_ref[...]

    def chunk(c, carry):
        r0 = pl.multiple_of(c * C, C)
        rs = pl.ds(r0, C)
        glr = misc_ref[rs, MISC_GLR:MISC_GLR + GLA_GATE_RANK]
        for h in range(GLA_HEADS):
            ks = slice(h * GLA_DK, (h + 1) * GLA_DK)
            vs = slice(h * GLA_DV, (h + 1) * GLA_DV)
            q = qk_ref[rs, ks] * (GLA_DK ** -0.5)
            k = qk_ref[rs, GLA_QK_W + h * GLA_DK:GLA_QK_W + (h + 1) * GLA_DK]
            v = v_ref[rs, vs]
            pre = jnp.dot(glr, a2_ref[:, ks], precision=hi, preferred_element_type=jnp.float32) + ab_ref[:, ks]
            la = _log_sigmoid(pre) / GLA_TAU
            b = jnp.dot(tri, la, precision=hi, preferred_element_type=jnp.float32)
            qd = (q * jnp.exp(b)).astype(MXU_DTYPE)
            kd = (k * jnp.exp(-b)).astype(MXU_DTYPE)
            vb = v.astype(MXU_DTYPE)
            a = lax.dot_general(qd, kd, (((1,), (1,)), ((), ())), preferred_element_type=jnp.float32)
            a = jnp.where(causal, a, 0.0)
            s_old = s_scr[h]
            o = (jnp.dot(a.astype(MXU_DTYPE), vb, preferred_element_type=jnp.float32)
                 + jnp.dot(qd, s_old.astype(MXU_DTYPE), preferred_element_type=jnp.float32))
            b_last = b[C - 1:C, :]
            kl_t = (k * jnp.exp(b_last - b)).T.astype(MXU_DTYPE)
            bl_t = jnp.dot(la.T, ones, precision=hi, preferred_element_type=jnp.float32)
            dec = jnp.exp(bl_t)
            s_scr[h] = (jnp.concatenate([dec] * (GLA_DV // LANES), axis=1) * s_old
                        + jnp.dot(kl_t, vb, preferred_element_type=jnp.float32))
            ms = jnp.mean(o * o, axis=-1, keepdims=True)
            on = o * lax.rsqrt(ms + EPS) * gn
            g = gr_ref[rs, vs]
            o_ref[rs, vs] = (on * (g * jax.nn.sigmoid(g))).astype(o_ref.dtype)
        return carry

    lax.fori_loop(0, rows // C, chunk, 0)

    @pl.when(t == pl.num_programs(1) - 1)
    def _():
        sfin_ref[0] = s_scr[...]


def _gla_prompt(z, a2, ab, gn, batch, seq):
    rows = _row_tile(seq, 512)
    nt = seq // rows
    zb = lambda off, w: pl.BlockSpec((rows, w), lambda b, t: (b * nt + t, off // w))
    full = lambda a: pl.BlockSpec(a.shape, lambda b, t: (0,) * a.ndim)
    return pl.pallas_call(
        functools.partial(_gla_prompt_kernel, rows=rows),
        grid=(batch, nt),
        in_specs=[zb(Z_GQK, 2 * GLA_QK_W), zb(Z_GV, GLA_V_W), zb(Z_GR, GLA_V_W), zb(Z_MISC, LANES),
                  full(a2), full(ab), full(gn)],
        out_specs=[pl.BlockSpec((rows, GLA_V_W), lambda b, t: (b * nt + t, 0)),
                   pl.BlockSpec((1, GLA_HEADS, GLA_DK, GLA_DV), lambda b, t: (b, 0, 0, 0))],
        out_shape=[jax.ShapeDtypeStruct((batch * seq, GLA_V_W), MXU_DTYPE),
                   jax.ShapeDtypeStruct((batch, GLA_HEADS, GLA_DK, GLA_DV), jnp.float32)],
        scratch_shapes=[pltpu.VMEM((GLA_HEADS, GLA_DK, GLA_DV), jnp.float32)],
        compiler_params=_cparams(("parallel", "arbitrary")),
        name="gla_prompt",
    )(z, z, z, z, a2, ab, gn)


def _gla_sample_kernel(qk_ref, v_ref, gr_ref, misc_ref, a2_ref, ab_ref, gn_ref, s0_ref, o_ref, s1_ref):
    hi = lax.Precision.HIGHEST
    eye = (lax.broadcasted_iota(jnp.int32, (GLA_DK, GLA_DK), 0)
           == lax.broadcasted_iota(jnp.int32, (GLA_DK, GLA_DK), 1))

    def col(row):
        return jnp.sum(jnp.where(eye, row, 0.0), axis=1, keepdims=True)

    glr = misc_ref[0][:, MISC_GLR:MISC_GLR + GLA_GATE_RANK]
    qk, v, gr = qk_ref[0], v_ref[0], gr_ref[0]
    for h in range(GLA_HEADS):
        ks = slice(h * GLA_DK, (h + 1) * GLA_DK)
        vs = slice(h * GLA_DV, (h + 1) * GLA_DV)
        q = qk[:, ks] * (GLA_DK ** -0.5)
        k = qk[:, GLA_QK_W + h * GLA_DK:GLA_QK_W + (h + 1) * GLA_DK]
        pre = jnp.dot(glr, a2_ref[:, ks], precision=hi, preferred_element_type=jnp.float32) + ab_ref[:, ks]
        b = _log_sigmoid(pre) / GLA_TAU
        s1 = col(jnp.exp(b)) * s0_ref[0, h] + col(k) * v[:, vs]
        s1_ref[0, h] = s1
        o = jnp.sum(col(q) * s1, axis=0, keepdims=True)
        ms = jnp.mean(o * o, axis=-1, keepdims=True)
        on = o * lax.rsqrt(ms + EPS) * gn_ref[...]
        g = gr[:, vs]
        o_ref[0, :, vs] = (on * (g * jax.nn.sigmoid(g))).astype(o_ref.dtype)


def _gla_sample(z, a2, ab, gn, s0):
    n = z.shape[0]
    z3 = z.reshape(n, 1, Z_W)
    zb = lambda off, w: pl.BlockSpec((1, 1, w), lambda b: (b, 0, off // w))
    full = lambda a: pl.BlockSpec(a.shape, lambda b: (0,) * a.ndim)
    sb = pl.BlockSpec((1, GLA_HEADS, GLA_DK, GLA_DV), lambda b: (b, 0, 0, 0))
    o, s1 = pl.pallas_call(
        _gla_sample_kernel,
        grid=(n,),
        in_specs=[zb(Z_GQK, 2 * GLA_QK_W), zb(Z_GV, GLA_V_W), zb(Z_GR, GLA_V_W), zb(Z_MISC, LANES),
                  full(a2), full(ab), full(gn), sb],
        out_specs=[pl.BlockSpec((1, 1, GLA_V_W), lambda b: (b, 0, 0)), sb],
        out_shape=[jax.ShapeDtypeStruct((n, 1, GLA_V_W), MXU_DTYPE),
                   jax.ShapeDtypeStruct(s0.shape, jnp.float32)],
        compiler_params=_cparams(("parallel",)),
        name="gla_sample",
    )(z3, z3, z3, z3, a2, ab, gn, s0)
    return o.reshape(n, GLA_V_W), s1


def _sortable(x):
    x = jnp.where(x == 0.0, 0.0, x)
    bits = pltpu.bitcast(x, jnp.int32)
    return jnp.where(bits < 0, bits ^ jnp.int32(0x7FFFFFFF), bits)


def _idx_select_kernel(iq_ref, misc_ref, ik_ref, mask_ref, hi_scr, lo_scr, *, tq, ck, seq, topk):
    i16, i32 = jnp.int16, jnp.int32
    min16 = -(2 ** 15)
    qi = pl.program_id(1)
    nck = ((qi + 1) * tq + ck - 1) // ck
    nsub = ck // LANES
    row = qi * tq + lax.broadcasted_iota(i32, (tq, 1), 0)
    w = misc_ref[:, MISC_IW:MISC_IW + IDX_HEADS]
    iq = iq_ref[...]
    iqh = [iq[:, h * IDX_DH:(h + 1) * IDX_DH] for h in range(IDX_HEADS)]
    wh = [w[:, h:h + 1] for h in range(IDX_HEADS)]

    def score_chunk(c, carry):
        c0 = pl.multiple_of(c * ck, ck)
        ks = ik_ref[pl.ds(c0, ck), :]
        acc = jnp.zeros((tq, ck), jnp.float32)
        for h in range(IDX_HEADS):
            s = lax.dot_general(iqh[h], ks, (((1,), (1,)), ((), ())), preferred_element_type=jnp.float32)
            acc = acc + wh[h] * jnp.maximum(s, 0.0)
        col = c0 + lax.broadcasted_iota(i32, (1, ck), 1)
        key = jnp.where(col <= row, _sortable(acc), INT_MIN)
        hi_scr[:, pl.ds(c0, ck)] = lax.shift_right_arithmetic(key, 16).astype(i16)
        lo_scr[:, pl.ds(c0, ck)] = (key ^ 0x8000).astype(i16)
        return carry

    lax.fori_loop(0, nck, score_chunk, 0)

    one, zero = jnp.ones((tq, LANES), i16), jnp.zeros((tq, LANES), i16)

    def bcast(x):
        return jnp.broadcast_to(x, (tq, LANES)).astype(i16)

    def count(pred):
        lane = lax.broadcasted_iota(i32, (tq, LANES), 1)

        def body(c, acc):
            c0 = pl.multiple_of(c * ck, ck)
            for u in range(nsub):
                sl = pl.ds(c0 + u * LANES, LANES)
                acc = acc + pred(hi_scr[:, sl], lo_scr[:, sl], (lane + (c0 + u * LANES)).astype(i16))
            return acc

        acc = lax.fori_loop(0, nck, body, zero)
        return jnp.sum(acc.astype(i32), axis=1, keepdims=True)

    def kth_largest(scr_pred, k):
        def step(i, ans):
            cand = ans + lax.shift_left(i32(1), i32(15) - i)
            cb = bcast(cand)
            cnt = count(lambda hi, lo, col: scr_pred(hi, lo, cb))
            return jnp.where(cnt >= k, cand, ans)
        return lax.fori_loop(0, 16, step, jnp.full((tq, 1), min16, i32))

    t_hi = jnp.maximum(kth_largest(lambda hi, lo, cb: jnp.where(hi >= cb, one, zero), topk), min16 + 1)
    tb = bcast(t_hi)
    c_hi = count(lambda hi, lo, col: jnp.where(hi > tb, one, zero))
    rem = topk - c_hi

    def bucket_chunk(c, carry):
        c0 = pl.multiple_of(c * ck, ck)
        for u in range(nsub):
            sl = pl.ds(c0 + u * LANES, LANES)
            lo_scr[:, sl] = jnp.where(hi_scr[:, sl] == tb, lo_scr[:, sl], jnp.full((tq, LANES), min16, i16))
        return carry

    lax.fori_loop(0, nck, bucket_chunk, 0)
    t_lo = kth_largest(lambda hi, lo, cb: jnp.where(lo >= cb, one, zero), rem)
    lb = bcast(t_lo)

    c_gt = count(lambda hi, lo, col: jnp.where(lo > lb, one, zero))
    c_eq = count(lambda hi, lo, col: jnp.where(hi == tb, jnp.where(lo == lb, one, zero), zero))
    need = rem - c_gt
    excess = c_eq > need

    def tie_limit():
        nbits = max(1, (seq - 1).bit_length())

        def jbit(i, jv):
            cand = jv | lax.shift_left(i32(1), i32(nbits - 1) - i)
            cb = bcast(cand)
            g = count(lambda hi, lo, col: jnp.where(hi == tb, jnp.where(lo == lb, jnp.where(col < cb, one, zero),
                                                                         zero), zero))
            return jnp.where(g < need, cand, jv)

        jv = lax.fori_loop(0, nbits, jbit, jnp.zeros((tq, 1), i32))
        return jnp.where(excess, jv, seq)

    jlim = lax.cond(jnp.max(jnp.where(excess, 1, 0)) > 0, tie_limit, lambda: jnp.full((tq, 1), seq, i32))
    jb = bcast(jlim)
    lane = lax.broadcasted_iota(i32, (tq, LANES), 1)

    def write_chunk(c, carry):
        c0 = pl.multiple_of(c * ck, ck)
        for u in range(nsub):
            sl = pl.ds(c0 + u * LANES, LANES)
            hi, lo = hi_scr[:, sl], lo_scr[:, sl]
            col = (lane + (c0 + u * LANES)).astype(i16)
            tie = jnp.where(lo == lb, jnp.where(col <= jb, one, zero), zero)
            sel = jnp.where(hi > tb, one, jnp.where(hi == tb, jnp.where(lo > lb, one, tie), zero))
            mask_ref[:, sl] = jnp.where(c < nck, sel, zero).astype(mask_ref.dtype)
        return carry

    lax.fori_loop(0, seq // ck, write_chunk, 0)


def _idx_select(iqb, misc2, ikb, batch, seq, topk):
    assert seq < 2 ** 15
    tq = _row_tile(seq, 256)
    ck = _row_tile(seq, 512)
    nq = seq // tq
    return pl.pallas_call(
        functools.partial(_idx_select_kernel, tq=tq, ck=ck, seq=seq, topk=topk),
        grid=(batch, nq),
        in_specs=[pl.BlockSpec((tq, IDX_Q_W), lambda b, q: (b * nq + q, 0)),
                  pl.BlockSpec((tq, LANES), lambda b, q: (b * nq + q, 0)),
                  pl.BlockSpec((seq, IDX_DH), lambda b, q: (b, 0))],
        out_specs=pl.BlockSpec((tq, seq), lambda b, q: (b * nq + q, 0)),
        out_shape=jax.ShapeDtypeStruct((batch * seq, seq), jnp.int8),
        scratch_shapes=[pltpu.VMEM((tq, seq), jnp.int16), pltpu.VMEM((tq, seq), jnp.int16)],
        compiler_params=_cparams(("parallel", "arbitrary")),
        name="idx_select",
    )(iqb, misc2, ikb)


def _attn_kernel(q_ref, k_ref, v_ref, m_ref, o_ref, qb_scr, acc_scr, m_scr, l_scr):
    qi, ki = pl.program_id(1), pl.program_id(2)
    t, tk = q_ref.shape[0], k_ref.shape[0]
    nsub = tk // LANES

    @pl.when(ki == 0)
    def _():
        qb_scr[...] = (q_ref[...] * (ATT_DH ** -0.5 * math.log2(math.e))).astype(qb_scr.dtype)
        acc_scr[...] = jnp.zeros_like(acc_scr)
        m_scr[...] = jnp.full(m_scr.shape, NEG, jnp.float32)
        l_scr[...] = jnp.zeros_like(l_scr)

    @pl.when(ki <= qi)
    def _():
        bias = jnp.where(m_ref[...].astype(jnp.int32) != 0, 0.0, NEG)
        ones = jnp.ones((tk, LANES), v_ref.dtype)
        for h in range(ATT_HEADS):
            sl = slice(h * ATT_DH, (h + 1) * ATT_DH)
            s = lax.dot_general(qb_scr[:, sl], k_ref[:, sl], (((1,), (1,)), ((), ())),
                                preferred_element_type=jnp.float32) + bias
            mx = s[:, 0:LANES]
            for u in range(1, nsub):
                mx = jnp.maximum(mx, s[:, u * LANES:(u + 1) * LANES])
            m_prev = m_scr[h]
            m_new = jnp.maximum(m_prev, jnp.max(mx, axis=1, keepdims=True))
            alpha = jnp.exp2(m_prev - m_new)
            p = jnp.exp2(s - jnp.concatenate([m_new] * nsub, axis=1)).astype(v_ref.dtype)
            pv = jnp.dot(p, jnp.concatenate([v_ref[:, sl], ones], axis=1), preferred_element_type=jnp.float32)
            m_scr[h] = m_new
            l_scr[h] = alpha * l_scr[h] + pv[:, ATT_DH:]
            acc_scr[:, sl] = alpha * acc_scr[:, sl] + pv[:, :ATT_DH]

    @pl.when(ki == qi)
    def _():
        for h in range(ATT_HEADS):
            sl = slice(h * ATT_DH, (h + 1) * ATT_DH)
            o_ref[:, sl] = (acc_scr[:, sl] / l_scr[h]).astype(o_ref.dtype)


def _attn(q, kb, vb, mask, batch, seq):
    t = _row_tile(seq, 512)
    nt = seq // t
    return pl.pallas_call(
        _attn_kernel,
        grid=(batch, nt, nt),
        in_specs=[pl.BlockSpec((t, ATT_W), lambda b, i, j: (b * nt + i, 0)),
                  pl.BlockSpec((t, ATT_W), lambda b, i, j: (b * nt + jnp.minimum(i, j), 0)),
                  pl.BlockSpec((t, ATT_W), lambda b, i, j: (b * nt + jnp.minimum(i, j), 0)),
                  pl.BlockSpec((t, t), lambda b, i, j: (b * nt + i, jnp.minimum(i, j)))],
        out_specs=pl.BlockSpec((t, ATT_W), lambda b, i, j: (b * nt + i, 0)),
        out_shape=jax.ShapeDtypeStruct((batch * seq, ATT_W), MXU_DTYPE),
        scratch_shapes=[pltpu.VMEM((t, ATT_W), MXU_DTYPE), pltpu.VMEM((t, ATT_W), jnp.float32),
                        pltpu.VMEM((ATT_HEADS, t, LANES), jnp.float32),
                        pltpu.VMEM((ATT_HEADS, t, LANES), jnp.float32)],
        compiler_params=_cparams(("parallel", "parallel", "arbitrary")),
        name="attn",
    )(q, kb, vb, mask)


def _sidx_kernel(pt_ref, iq_ref, w_ref, ikn_ref, cache_ref, out_ref, buf, sem, *, n_pages):
    b = pl.program_id(0)
    nb = pl.num_programs(0)
    past = n_pages * PAGE_SIZE

    def page_copy(bb, slot, p):
        return pltpu.make_async_copy(cache_ref.at[pt_ref[bb, p]],
                                     buf.at[slot, pl.ds(p * PAGE_SIZE, PAGE_SIZE)], sem.at[slot])

    def start(bb, slot):
        for p in range(n_pages):
            page_copy(bb, slot, p).start()

    @pl.when(b == 0)
    def _():
        start(0, 0)

    @pl.when(b + 1 < nb)
    def _():
        start(b + 1, (b + 1) % 2)

    slot = b % 2
    for p in range(n_pages):
        page_copy(b, slot, p).wait()

    iq = iq_ref[0]
    w = w_ref[0]
    ik = buf[slot].astype(iq.dtype)
    s = lax.dot_general(iq, ik, (((1,), (1,)), ((), ())), preferred_element_type=jnp.float32)
    sc = jnp.sum(w * jnp.maximum(s, 0.0), axis=0, keepdims=True)
    out_ref[0, :, 0:past] = jnp.where(sc == 0.0, 0.0, sc)
    s_new = jnp.sum(iq.astype(jnp.float32) * ikn_ref[0].astype(iq.dtype).astype(jnp.float32), axis=1, keepdims=True)
    sc_new = jnp.sum(w * jnp.maximum(s_new, 0.0), axis=0, keepdims=True)
    sc_new = jnp.where(sc_new == 0.0, 0.0, sc_new)
    lane = lax.broadcasted_iota(jnp.int32, (1, LANES), 1)
    out_ref[0, :, past:past + LANES] = jnp.where(lane == 0, sc_new, -jnp.inf)


def _sidx(page_table, iq3, w3, ikn3, cache_idx):
    bd, n_pages = page_table.shape
    past = n_pages * PAGE_SIZE
    grid_spec = pltpu.PrefetchScalarGridSpec(
        num_scalar_prefetch=1,
        grid=(bd,),
        in_specs=[pl.BlockSpec((1, IDX_HEADS, IDX_DH), lambda b, pt: (b, 0, 0)),
                  pl.BlockSpec((1, IDX_HEADS, 1), lambda b, pt: (b, 0, 0)),
                  pl.BlockSpec((1, 1, IDX_DH), lambda b, pt: (b, 0, 0)),
                  pl.BlockSpec(memory_space=pl.ANY)],
        out_specs=pl.BlockSpec((1, 1, past + LANES), lambda b, pt: (b, 0, 0)),
        scratch_shapes=[pltpu.VMEM((2, past, IDX_DH), jnp.float32), pltpu.SemaphoreType.DMA((2,))],
    )
    return pl.pallas_call(
        functools.partial(_sidx_kernel, n_pages=n_pages),
        grid_spec=grid_spec,
        out_shape=jax.ShapeDtypeStruct((bd, 1, past + LANES), jnp.float32),
        compiler_params=_cparams(("arbitrary",)),
        name="sidx",
    )(page_table, iq3, w3, ikn3, cache_idx)


def _stopk_kernel(sc_ref, sel_ref, x_scr, *, topk):
    x_scr[...] = sc_ref[...]
    shape = x_scr.shape
    big = jnp.int32(2 ** 30)

    def body(it, sel):
        x = x_scr[...]
        col = lax.broadcasted_iota(jnp.int32, shape, 1)
        m = jnp.max(x, axis=1, keepdims=True)
        idx = jnp.min(jnp.where(x == m, col, big), axis=1, keepdims=True)
        x_scr[...] = jnp.where(col == idx, -jnp.inf, x)
        kcol = lax.broadcasted_iota(jnp.int32, sel.shape, 1)
        return jnp.where(kcol == it, idx, sel)

    sel_ref[...] = lax.fori_loop(0, topk, body, jnp.zeros(sel_ref.shape, jnp.int32))


def _stopk(scores, topk):
    bd, lp = scores.shape
    return pl.pallas_call(
        functools.partial(_stopk_kernel, topk=topk),
        out_shape=jax.ShapeDtypeStruct((bd, topk), jnp.int32),
        scratch_shapes=[pltpu.VMEM((bd, lp), jnp.float32)],
        compiler_params=pltpu.CompilerParams(vmem_limit_bytes=VMEM_LIMIT),
        name="stopk",
    )(scores)


def _sattn_kernel(sel_ref, pt_ref, q_ref, kn_ref, vn_ref, selv_ref, ck_ref, cv_ref, o_ref, kbuf, vbuf, sem,
                  *, topk, past):
    b = pl.program_id(0)
    nb = pl.num_programs(0)

    def row_copies(bb, slot, j):
        p = jnp.minimum(sel_ref[bb, j], past - 1)
        phys = pt_ref[bb, p // PAGE_SIZE]
        r = p % PAGE_SIZE
        return (pltpu.make_async_copy(ck_ref.at[phys, r], kbuf.at[slot, j], sem.at[0, slot]),
                pltpu.make_async_copy(cv_ref.at[phys, r], vbuf.at[slot, j], sem.at[1, slot]))

    def start(bb, slot):
        def body(j, c):
            ck, cv = row_copies(bb, slot, j)
            ck.start()
            cv.start()
            return c
        lax.fori_loop(0, topk, body, 0)

    @pl.when(b == 0)
    def _():
        start(0, 0)

    @pl.when(b + 1 < nb)
    def _():
        start(b + 1, (b + 1) % 2)

    slot = b % 2

    def wbody(j, c):
        ck, cv = row_copies(b, slot, j)
        ck.wait()
        cv.wait()
        return c

    lax.fori_loop(0, topk, wbody, 0)

    selv = selv_ref[0]
    in_past = selv < past
    q = q_ref[...]
    kg = jnp.where(in_past, kbuf[slot], kn_ref[...])
    vg = jnp.where(in_past, vbuf[slot], vn_ref[...])
    s = jnp.sum(kg * q, axis=-1, keepdims=True) * (ATT_DH ** -0.5)
    s = jnp.where(selv <= past, s, -jnp.inf)
    m = jnp.max(s, axis=0, keepdims=True)
    p = jnp.exp(s - m)
    p = p / jnp.sum(p, axis=0, keepdims=True)
    o_ref[...] = jnp.sum(p * vg, axis=0, keepdims=True).astype(o_ref.dtype)


def _sattn(sel, page_table, q3, kn3, vn3, cache_k, cache_v):
    bd, topk = sel.shape
    past = page_table.shape[1] * PAGE_SIZE
    hb = pl.BlockSpec((1, ATT_HEADS, ATT_DH), lambda b, s, pt: (b, 0, 0))
    grid_spec = pltpu.PrefetchScalarGridSpec(
        num_scalar_prefetch=2,
        grid=(bd,),
        in_specs=[hb, hb, hb, pl.BlockSpec((1, topk, 1, 1), lambda b, s, pt: (b, 0, 0, 0)),
                  pl.BlockSpec(memory_space=pl.ANY), pl.BlockSpec(memory_space=pl.ANY)],
        out_specs=hb,
        scratch_shapes=[pltpu.VMEM((2, topk, ATT_HEADS, ATT_DH), jnp.float32),
                        pltpu.VMEM((2, topk, ATT_HEADS, ATT_DH), jnp.float32),
                        pltpu.SemaphoreType.DMA((2, 2))],
    )
    return pl.pallas_call(
        functools.partial(_sattn_kernel, topk=topk, past=past),
        grid_spec=grid_spec,
        out_shape=jax.ShapeDtypeStruct((bd, ATT_HEADS, ATT_DH), MXU_DTYPE),
        compiler_params=_cparams(("arbitrary",)),
        name="sattn",
    )(sel, page_table, q3, kn3, vn3, sel.reshape(bd, topk, 1, 1), cache_k, cache_v)


def _merge_kernel(og_ref, oa_ref, ga_ref, gb_ref, x_ref, wg_ref, wa_ref, wo_ref, nf_ref, x1_ref, hn_ref):
    f32 = jnp.float32
    mix = (jax.nn.sigmoid(ga_ref[...]) * jnp.dot(og_ref[...], wg_ref[...], preferred_element_type=f32)
           + jax.nn.sigmoid(gb_ref[...]) * jnp.dot(oa_ref[...], wa_ref[...], preferred_element_type=f32))
    x1 = x_ref[...] + jnp.dot(mix.astype(wo_ref.dtype), wo_ref[...], preferred_element_type=f32)
    x1_ref[...] = x1
    ms = jnp.mean(x1 * x1, axis=-1, keepdims=True)
    hn_ref[...] = (x1 * lax.rsqrt(ms + EPS) * nf_ref[...]).astype(hn_ref.dtype)


def _merge(og, oa, z, x, wg, wa, wo, nf):
    n = x.shape[0]
    tm = _row_tile(n, 512)
    rb = lambda: pl.BlockSpec((tm, D_MODEL), lambda i: (i, 0))
    zb = lambda off: pl.BlockSpec((tm, D_MODEL), lambda i: (i, off // D_MODEL))
    wb = lambda: pl.BlockSpec((D_MODEL, D_MODEL), lambda i: (0, 0))
    return pl.pallas_call(
        _merge_kernel,
        grid=(n // tm,),
        in_specs=[rb(), rb(), zb(Z_GA), zb(Z_GB), rb(), wb(), wb(), wb(), pl.BlockSpec((1, D_MODEL), lambda i: (0, 0))],
        out_specs=[rb(), rb()],
        out_shape=[jax.ShapeDtypeStruct((n, D_MODEL), jnp.float32), jax.ShapeDtypeStruct((n, D_MODEL), MXU_DTYPE)],
        compiler_params=_cparams(("parallel",)),
        name="merge",
    )(og, oa, z, z, x, wg, wa, wo, nf)


_PAIRS = [(r1, r2) for r1 in range(PEER_TOPK) for r2 in range(PEER_TOPK) if (r1 + 1) * (r2 + 1) <= PEER_TOPK]


def _top16_sorted(x, exact_ties):
    nk = x.shape[0]
    rid = lax.broadcasted_iota(jnp.int32, x.shape, 0)
    rank = jnp.full(x.shape, PEER_TOPK, jnp.int32)
    vals = []
    for r in range(PEER_TOPK):
        m = jnp.max(x, axis=0, keepdims=True)
        hit = x == m
        if exact_ties:
            hit = rid == jnp.min(jnp.where(hit, rid, nk), axis=0, keepdims=True)
        x = jnp.where(hit, -jnp.inf, x)
        rank = jnp.where(hit, r, rank)
        vals.append(m)
    return vals, rank


def _router_body(hn_ref, wqt_ref, keys_ref, r2_ref, e2_ref, cnt_ref, e1_ref, exact_ties):
    f32 = jnp.float32
    qt = lax.dot_general(wqt_ref[...], hn_ref[...], (((1,), (1,)), ((), ())), preferred_element_type=f32)
    qt = qt.astype(MXU_DTYPE)
    half = PEER_DKEY // 2
    npair = len(_PAIRS)
    ok = jnp.zeros((1, hn_ref.shape[0]), jnp.int32)
    for h in range(PEER_HEADS):
        s1 = jnp.dot(keys_ref[2 * h], qt[(2 * h) * half:(2 * h + 1) * half, :], preferred_element_type=f32)
        s2 = jnp.dot(keys_ref[2 * h + 1], qt[(2 * h + 1) * half:(2 * h + 2) * half, :], preferred_element_type=f32)
        v1, rank1 = _top16_sorted(s1, exact_ties)
        v2, rank2 = _top16_sorted(s2, exact_ties)
        sel1 = rank1 < PEER_TOPK
        sel2 = rank2 < PEER_TOPK
        n12 = (jnp.sum(jnp.where(sel1, 1, 0), axis=0, keepdims=True)
               + jnp.sum(jnp.where(sel2, 1, 0), axis=0, keepdims=True))
        ok = jnp.where(n12 == 2 * PEER_TOPK, ok, -1)
        cand = jnp.concatenate([v1[a] + v2[b] for a, b in _PAIRS], axis=0)
        pid = lax.broadcasted_iota(jnp.int32, cand.shape, 0)
        c = cand
        picked = jnp.zeros(cand.shape, jnp.int32)
        for _ in range(PEER_TOPK):
            m = jnp.max(c, axis=0, keepdims=True)
            hit = pid == jnp.min(jnp.where(c == m, pid, npair), axis=0, keepdims=True)
            c = jnp.where(hit, -jnp.inf, c)
            picked = jnp.where(hit, 1, picked)
        top = v1[0] + v2[0]
        zsum = jnp.sum(jnp.where(picked > 0, jnp.exp(cand - top), 0.0), axis=0, keepdims=True)
        half_inv_z = 0.5 / zsum
        cnt = jnp.zeros(s1.shape, f32)
        for a in range(PEER_TOPK):
            rows = [i for i, (pa, _) in enumerate(_PAIRS) if pa == a]
            n_a = jnp.sum(picked[rows[0]:rows[-1] + 1], axis=0, keepdims=True).astype(f32)
            cnt = jnp.where(rank1 == a, n_a, cnt)
        cnt_ref[0, h] = cnt
        e1_ref[0, h] = jnp.where(sel1, jnp.exp(s1 - v1[0]) * half_inv_z, 0.0)
        r2_ref[0, h] = rank2.astype(f32).astype(r2_ref.dtype)
        e2_ref[0, h] = jnp.where(sel2, jnp.exp(s2 - v2[0]), 0.0).astype(e2_ref.dtype)
    return ok


def _router_kernel(hn_ref, wqt_ref, keys_ref, r2_ref, e2_ref, cnt_ref, e1_ref):
    outs = (r2_ref, e2_ref, cnt_ref, e1_ref)
    ok = _router_body(hn_ref, wqt_ref, keys_ref, *outs, exact_ties=False)

    @pl.when(jnp.min(ok) < 0)
    def _():
        _router_body(hn_ref, wqt_ref, keys_ref, *outs, exact_ties=True)


def _router(hn, wqt, keys, tm):
    n = hn.shape[0]
    nt = n // tm
    tr = min(tm, LANES)
    per = tm // tr
    shp = lambda dt: jax.ShapeDtypeStruct((nt, PEER_HEADS, PEER_NKEYS, tm), dt)
    ob = pl.BlockSpec((1, PEER_HEADS, PEER_NKEYS, tr), lambda i: (i // per, 0, 0, i % per))
    return pl.pallas_call(
        _router_kernel,
        grid=(n // tr,),
        in_specs=[pl.BlockSpec((tr, D_MODEL), lambda i: (i, 0)),
                  pl.BlockSpec((D_MODEL, D_MODEL), lambda i: (0, 0)),
                  pl.BlockSpec(keys.shape, lambda i: (0, 0, 0))],
        out_specs=[ob, ob, ob, ob],
        out_shape=[shp(GATE_DTYPE), shp(GATE_DTYPE), shp(jnp.float32), shp(jnp.float32)],
        compiler_params=_cparams(("parallel",)),
        name="router",
    )(hn, wqt, keys)


def _gelu2(x):
    return x * (1.0 + lax.erf(x * (2.0 ** -0.5)))


def _peer_kernel(hn_ref, u_ref, vt_ref, r2_ref, e2_ref, cnt_ref, e1_ref, x1_ref, y_ref, acc_scr, p_scr, *, rows1):
    j = pl.program_id(1)
    f32 = jnp.float32
    gd = r2_ref.dtype
    tm = hn_ref.shape[0]
    sub = 16
    grp = PEER_NKEYS // sub

    @pl.when(j == 0)
    def _():
        acc_scr[...] = jnp.zeros_like(acc_scr)

    at = lax.dot_general(u_ref[...], hn_ref[...], (((1,), (1,)), ((), ())), preferred_element_type=f32)
    for c in range(rows1):
        w = jnp.zeros((grp, sub, tm), gd)
        for h in range(PEER_HEADS):
            cnt = jnp.broadcast_to(cnt_ref[0, h, c:c + 1, :], (sub, tm)).astype(gd)[None]
            e1 = jnp.broadcast_to(e1_ref[0, h, c:c + 1, :], (sub, tm)).astype(gd)[None]
            r2 = r2_ref[0, h].reshape(grp, sub, tm)
            e2 = e2_ref[0, h].reshape(grp, sub, tm)
            w = w + jnp.where(r2 < cnt, e2, jnp.zeros_like(e2)) * e1
        rs = slice(c * PEER_NKEYS, (c + 1) * PEER_NKEYS)
        p_scr[rs, :] = w.reshape(PEER_NKEYS, tm) * _gelu2(at[rs, :]).astype(gd)
    acc_scr[...] += jnp.dot(vt_ref[...], p_scr[...], preferred_element_type=f32)

    @pl.when(j == pl.num_programs(1) - 1)
    def _():
        y_ref[...] = x1_ref[...] + acc_scr[...].T


def _peer(hn, u, vt, router_out, x1, tm):
    n = hn.shape[0]
    nt = n // tm
    rows1 = 16
    te = rows1 * PEER_NKEYS
    ne = (PEER_NKEYS * PEER_NKEYS) // te
    r2, e2, cnt, e1 = router_out
    full = pl.BlockSpec((1, PEER_HEADS, PEER_NKEYS, tm), lambda i, j: (i, 0, 0, 0))
    part = pl.BlockSpec((1, PEER_HEADS, rows1, tm), lambda i, j: (i, 0, j, 0))
    return pl.pallas_call(
        functools.partial(_peer_kernel, rows1=rows1),
        grid=(nt, ne),
        in_specs=[pl.BlockSpec((tm, D_MODEL), lambda i, j: (i, 0)),
                  pl.BlockSpec((te, D_MODEL), lambda i, j: (j, 0)),
                  pl.BlockSpec((D_MODEL, te), lambda i, j: (0, j)),
                  full, full, part, part,
                  pl.BlockSpec((tm, D_MODEL), lambda i, j: (i, 0))],
        out_specs=pl.BlockSpec((tm, D_MODEL), lambda i, j: (i, 0)),
        out_shape=jax.ShapeDtypeStruct((n, D_MODEL), jnp.float32),
        scratch_shapes=[pltpu.VMEM((D_MODEL, tm), jnp.float32), pltpu.VMEM((te, tm), MXU_DTYPE)],
        compiler_params=_cparams(("parallel", "arbitrary")),
        name="peer",
    )(hn, u, vt, r2, e2, cnt, e1, x1)


def _transpose_cast_kernel(x_ref, o_ref):
    o_ref[...] = x_ref[...].T.astype(o_ref.dtype)


def _transpose_cast(x, dtype):
    r, c = x.shape
    tr = _row_tile(r, 512)
    return pl.pallas_call(
        _transpose_cast_kernel,
        grid=(r // tr,),
        in_specs=[pl.BlockSpec((tr, c), lambda i: (i, 0))],
        out_specs=pl.BlockSpec((c, tr), lambda i: (0, i)),
        out_shape=jax.ShapeDtypeStruct((c, r), dtype),
        compiler_params=_cparams(("parallel",)),
        name="transpose_cast",
    )(x)


def _reorder_w_in(w_in):
    sizes = (GLA_QK_W, GLA_QK_W, GLA_V_W, GLA_V_W, GLA_GATE_RANK, ATT_W, ATT_W, ATT_W, IDX_Q_W, IDX_DH, IDX_HEADS,
             D_MODEL, D_MODEL)
    offs = [0]
    for s in sizes:
        offs.append(offs[-1] + s)
    gq, gk, gv, gr, glr, aq, ak, av, iq, ik, iw, ga, gb = (w_in[:, offs[i]:offs[i + 1]] for i in range(len(sizes)))
    pad = jnp.zeros((w_in.shape[0], LANES - IDX_DH - IDX_HEADS - GLA_GATE_RANK), w_in.dtype)
    return jnp.concatenate([gq, gk, gv, gr, aq, ak, av, ga, gb, iq, ik, iw, glr, pad], axis=1)


def _ffn(og, oa, z, x, wg, wa, wo, nf, wqt, keys, u, vt):
    n = x.shape[0]
    x1, hn = _merge(og, oa, z, x, wg, wa, wo, nf)
    tm = _row_tile(n, 512)
    r = _router(hn, wqt, keys, tm)
    return _peer(hn, u, vt, r, x1, tm)


def kernel(x_prompt, x_sample, cache_k, cache_v, cache_idx_k, state_gla, page_table, norm_mix, w_in, gla_a2,
           gla_a_bias, q_norm, k_norm, gla_norm, w_br_gla, w_br_att, w_out, norm_ffn, peer_wq, peer_keys,
           peer_u, peer_v):
    depth = w_in.shape[0]
    assert depth == 1 and x_sample.shape[1] == 1
    B, S, _ = x_prompt.shape
    Bd = x_sample.shape[0]
    n_pages = page_table.shape[1]
    past = n_pages * PAGE_SIZE
    bf = MXU_DTYPE
    l = 0

    w_all = _reorder_w_in(w_in[l]).astype(bf)
    g_mix = norm_mix[l].reshape(1, D_MODEL)
    a2, ab = gla_a2[l], gla_a_bias[l].reshape(1, GLA_QK_W)
    gn = gla_norm[l].reshape(1, GLA_DV)
    qn, kn = q_norm[l].reshape(1, ATT_DH), k_norm[l].reshape(1, ATT_DH)
    wg, wa, wo = w_br_gla[l].astype(bf), w_br_att[l].astype(bf), w_out[l].astype(bf)
    nf = norm_ffn[l].reshape(1, D_MODEL)
    wqt = _transpose_cast(peer_wq[l], bf)
    keys = peer_keys[l].reshape(PEER_HEADS * 2, PEER_NKEYS, PEER_DKEY // 2).astype(bf)
    u = peer_u[l].astype(bf)
    vt = _transpose_cast(peer_v[l], bf)
    ffn_w = (wg, wa, wo, nf, wqt, keys, u, vt)

    xp = x_prompt.reshape(B * S, D_MODEL)
    zp = _proj(xp, g_mix, w_all)
    tabs_p = _rope_tables(jnp.arange(S))
    qp, kp, kpb, vpb, iqpb, miscp, ikpb = _post(zp, tabs_p, S, qn, kn)
    ogp, sp = _gla_prompt(zp, a2, ab, gn, B, S)
    topk_p = min(TOPK_MAX, S // 4)
    maskp = _idx_select(iqpb, miscp, ikpb, B, S, topk_p)
    oap = _attn(qp, kpb, vpb, maskp, B, S)
    yp = _ffn(ogp, oap, zp, xp, *ffn_w)

    xs = x_sample.reshape(Bd, D_MODEL)
    zs = _proj(xs, g_mix, w_all)
    tabs_s = _rope_tables(jnp.full((Bd,), past, jnp.int32))
    qs, ks, _, _, iqsb, miscs, iksb = _post(zs, tabs_s, Bd, qn, kn)
    vs = zs[:, Z_AV:Z_AV + ATT_W]
    ogs, ss = _gla_sample(zs, a2, ab, gn, state_gla[l])
    scores = _sidx(page_table, iqsb.reshape(Bd, IDX_HEADS, IDX_DH),
                   miscs[:, MISC_IW:MISC_IW + IDX_HEADS].reshape(Bd, IDX_HEADS, 1),
                   iksb.reshape(Bd, 1, IDX_DH), cache_idx_k[l])
    topk_s = min(TOPK_MAX, (past + 1) // 4)
    sel = _stopk(scores.reshape(Bd, past + LANES), topk_s)
    hd = (Bd, ATT_HEADS, ATT_DH)
    oas = _sattn(sel, page_table, qs.reshape(hd), ks.reshape(hd), vs.reshape(hd), cache_k[l], cache_v[l])
    ys = _ffn(ogs, oas.reshape(Bd, ATT_W), zs, xs, *ffn_w)

    return (yp.reshape(B, S, D_MODEL), ys.reshape(Bd, 1, D_MODEL),
            kp.reshape(1, B, S, ATT_HEADS, ATT_DH), zp[:, Z_AV:Z_AV + ATT_W].reshape(1, B, S, ATT_HEADS, ATT_DH),
            miscp[:, :IDX_DH].reshape(1, B, S, IDX_DH), sp[None],
            ks.reshape(1, Bd, 1, ATT_HEADS, ATT_DH), vs.reshape(1, Bd, 1, ATT_HEADS, ATT_DH),
            miscs[:, :IDX_DH].reshape(1, Bd, 1, IDX_DH), ss[None])
```

```python
import functools
import math

import jax
import jax.numpy as jnp
from jax import lax
from jax.experimental import pallas as pl
from jax.experimental.pallas import tpu as pltpu

D_MODEL = 1024
GLA_HEADS, GLA_DK, GLA_DV = 4, 128, 256
GLA_GATE_RANK = 16
GLA_TAU = 16.0
GLA_CHUNK = 64
ATT_HEADS, ATT_DH = 8, 128
IDX_HEADS, IDX_DH = 8, 64
TOPK_MAX = 256
PAGE_SIZE = 128
ROPE_THETA = 10000.0
PEER_HEADS, PEER_NKEYS, PEER_DKEY, PEER_TOPK = 8, 128, 128, 16
EPS = 1e-6

GLA_QK_W = GLA_HEADS * GLA_DK
GLA_V_W = GLA_HEADS * GLA_DV
ATT_W = ATT_HEADS * ATT_DH
IDX_Q_W = IDX_HEADS * IDX_DH

LANES = 128
VMEM_LIMIT = 56 * 1024 * 1024
MXU_DTYPE = jnp.bfloat16
GATE_DTYPE = jnp.bfloat16
NEG = -1e30
INT_MIN = -(2 ** 31)

Z_GQK, Z_GV, Z_GR, Z_AQ, Z_AK, Z_AV, Z_GA, Z_GB = (i * 1024 for i in range(8))
Z_IQ = 8 * 1024
Z_MISC = Z_IQ + IDX_Q_W
Z_W = Z_MISC + LANES
MISC_IK, MISC_IW, MISC_GLR = 0, IDX_DH, IDX_DH + IDX_HEADS


def _cparams(sem):
    return pltpu.CompilerParams(dimension_semantics=sem, vmem_limit_bytes=VMEM_LIMIT)


def _row_tile(n, pref):
    t = min(pref, n)
    while n % t:
        t //= 2
    return t


def _proj_kernel(x_ref, g_ref, w_ref, z_ref, h_scr):
    @pl.when(pl.program_id(1) == 0)
    def _():
        x = x_ref[...]
        ms = jnp.mean(x * x, axis=-1, keepdims=True)
        h_scr[...] = (x * lax.rsqrt(ms + EPS) * g_ref[...]).astype(h_scr.dtype)

    z_ref[...] = jnp.dot(h_scr[...], w_ref[...], preferred_element_type=jnp.float32)


def _proj(x, g, w):
    n = x.shape[0]
    tm = _row_tile(n, 512)
    nblk = 3
    tn = Z_W // nblk
    return pl.pallas_call(
        _proj_kernel,
        grid=(n // tm, nblk),
        in_specs=[pl.BlockSpec((tm, D_MODEL), lambda i, j: (i, 0)),
                  pl.BlockSpec((1, D_MODEL), lambda i, j: (0, 0)),
                  pl.BlockSpec((D_MODEL, tn), lambda i, j: (0, j))],
        out_specs=pl.BlockSpec((tm, tn), lambda i, j: (i, j)),
        out_shape=jax.ShapeDtypeStruct((n, Z_W), jnp.float32),
        scratch_shapes=[pltpu.VMEM((tm, D_MODEL), MXU_DTYPE)],
        compiler_params=_cparams(("parallel", "arbitrary")),
        name="proj",
    )(x, g, w)


def _post_kernel(aq_ref, ak_ref, av_ref, iq_ref, misc_ref, c128_ref, s128_ref, c64_ref, s64_ref,
                 qn_ref, kn_ref, q_out, k_out, kb_out, vb_out, iqb_out, misc_out, ikb_out):
    c128, s128 = c128_ref[...], s128_ref[...]

    def norm_rope(x, g):
        ms = jnp.mean(x * x, axis=-1, keepdims=True)
        y = x * lax.rsqrt(ms + EPS) * g
        return y * c128 + pltpu.roll(y, ATT_DH // 2, 1) * s128

    for h in range(ATT_HEADS):
        sl = slice(h * ATT_DH, (h + 1) * ATT_DH)
        q_out[:, sl] = norm_rope(aq_ref[:, sl], qn_ref[...])
        kk = norm_rope(ak_ref[:, sl], kn_ref[...])
        k_out[:, sl] = kk
        kb_out[:, sl] = kk.astype(kb_out.dtype)
    vb_out[...] = av_ref[...].astype(vb_out.dtype)

    half = IDX_DH // 2
    c64, s64 = c64_ref[...], s64_ref[...]

    def rope64(x, reps):
        w = x.shape[1]
        lane = lax.broadcasted_iota(jnp.int32, x.shape, 1)
        first = (lane % IDX_DH) < half
        partner = jnp.where(first, pltpu.roll(x, w - half, 1), pltpu.roll(x, half, 1))
        c = jnp.concatenate([c64] * reps, axis=1) if reps > 1 else c64
        s = jnp.concatenate([s64] * reps, axis=1) if reps > 1 else s64
        return x * c + partner * s

    iq = rope64(iq_ref[...], IDX_Q_W // LANES) * (IDX_DH ** -0.5)
    iqb_out[...] = iq.astype(iqb_out.dtype)
    misc = misc_ref[...]
    rot = rope64(misc, 1)
    lane = lax.broadcasted_iota(jnp.int32, misc.shape, 1)
    out = jnp.where(lane < MISC_IW, rot,
                    jnp.where(lane < MISC_GLR, misc * (IDX_HEADS ** -0.5), misc))
    misc_out[...] = out
    ikb_out[...] = rot[:, :IDX_DH].astype(ikb_out.dtype)


def _post(z, tabs, tab_rows, q_norm, k_norm):
    n = z.shape[0]
    tm = _row_tile(min(n, tab_rows), 512)
    ntab = tab_rows // tm
    c128, s128, c64, s64 = tabs
    zb = lambda off, w: pl.BlockSpec((tm, w), lambda i: (i, off // w))
    tb = pl.BlockSpec((tm, LANES), lambda i: (i % ntab, 0))
    gb = pl.BlockSpec((1, ATT_DH), lambda i: (0, 0))
    ob = lambda w: pl.BlockSpec((tm, w), lambda i: (i, 0))
    f32, bf = jnp.float32, MXU_DTYPE
    return pl.pallas_call(
        _post_kernel,
        grid=(n // tm,),
        in_specs=[zb(Z_AQ, ATT_W), zb(Z_AK, ATT_W), zb(Z_AV, ATT_W), zb(Z_IQ, IDX_Q_W), zb(Z_MISC, LANES),
                  tb, tb, tb, tb, gb, gb],
        out_specs=[ob(ATT_W), ob(ATT_W), ob(ATT_W), ob(ATT_W), ob(IDX_Q_W), ob(LANES), ob(IDX_DH)],
        out_shape=[jax.ShapeDtypeStruct((n, ATT_W), f32), jax.ShapeDtypeStruct((n, ATT_W), f32),
                   jax.ShapeDtypeStruct((n, ATT_W), bf), jax.ShapeDtypeStruct((n, ATT_W), bf),
                   jax.ShapeDtypeStruct((n, IDX_Q_W), bf), jax.ShapeDtypeStruct((n, LANES), f32),
                   jax.ShapeDtypeStruct((n, IDX_DH), bf)],
        compiler_params=_cparams(("parallel",)),
        name="post",
    )(z, z, z, z, z, c128, s128, c64, s64, q_norm, k_norm)


def _rope_tables(pos):
    def tab(dh):
        half = dh // 2
        inv = ROPE_THETA ** (-jnp.arange(half, dtype=jnp.float32) / half)
        ang = pos.astype(jnp.float32)[:, None] * inv[None, :]
        c, s = jnp.cos(ang), jnp.sin(ang)
        reps = LANES // dh
        return (jnp.concatenate([c, c] * reps, axis=1), jnp.concatenate([-s, s] * reps, axis=1))
    c128, s128 = tab(ATT_DH)
    c64, s64 = tab(IDX_DH)
    return c128, s128, c64, s64


def _log_sigmoid(x):
    return jnp.minimum(x, 0.0) - jnp.log(1.0 + jnp.exp(-jnp.abs(x)))


def _gla_prompt_kernel(qk_ref, v_ref, gr_ref, misc_ref, a2_ref, ab_ref, gn_ref, o_ref, sfin_ref, s_scr, *, rows):
    t = pl.program_id(1)
    C = GLA_CHUNK
    hi = lax.Precision.HIGHEST

    @pl.when(t == 0)
    def _():
        s_scr[...] = jnp.zeros_like(s_scr)

    ri = lax.broadcasted_iota(jnp.int32, (C, C), 0)
    ci = lax.broadcasted_iota(jnp.int32, (C, C), 1)
    causal = ri >= ci
    tri = causal.astype(jnp.float32)
    ones = jnp.ones((C, GLA_DK), jnp.float32)
    gn = gn_ref[...]

    def chunk(c, carry):
        r0 = pl.multiple_of(c * C, C)
        rs = pl.ds(r0, C)
        glr = misc_ref[rs, MISC_GLR:MISC_GLR + GLA_GATE_RANK]
        for h in range(GLA_HEADS):
            ks = slice(h * GLA_DK, (h + 1) * GLA_DK)
            vs = slice(h * GLA_DV, (h + 1) * GLA_DV)
            q = qk_ref[rs, ks] * (GLA_DK ** -0.5)
            k = qk_ref[rs, GLA_QK_W + h * GLA_DK:GLA_QK_W + (h + 1) * GLA_DK]
            v = v_ref[rs, vs]
            pre = jnp.dot(glr, a2_ref[:, ks], precision=hi, preferred_element_type=jnp.float32) + ab_ref[:, ks]
            la = _log_sigmoid(pre) / GLA_TAU
            b = jnp.dot(tri, la, precision=hi, preferred_element_type=jnp.float32)
            qd = (q * jnp.exp(b)).astype(MXU_DTYPE)
            kd = (k * jnp.exp(-b)).astype(MXU_DTYPE)
            vb = v.astype(MXU_DTYPE)
            a = lax.dot_general(qd, kd, (((1,), (1,)), ((), ())), preferred_element_type=jnp.float32)
            a = jnp.where(causal, a, 0.0)
            s_old = s_scr[h]
            o = (jnp.dot(a.astype(MXU_DTYPE), vb, preferred_element_type=jnp.float32)
                 + jnp.dot(qd, s_old.astype(MXU_DTYPE), preferred_element_type=jnp.float32))
            b_last = b[C - 1:C, :]
            kl_t = (k * jnp.exp(b_last - b)).T.astype(MXU_DTYPE)
            bl_t = jnp.dot(la.T, ones, precision=hi, preferred_element_type=jnp.float32)
            dec = jnp.exp(bl_t)
            s_scr[h] = (jnp.concatenate([dec] * (GLA_DV // LANES), axis=1) * s_old
                        + jnp.dot(kl_t, vb, preferred_element_type=jnp.float32))
            ms = jnp.mean(o * o, axis=-1, keepdims=True)
            on = o * lax.rsqrt(ms + EPS) * gn
            g = gr_ref[rs, vs]
            o_ref[rs, vs] = (on * (g * jax.nn.sigmoid(g))).astype(o_ref.dtype)
        return carry

    lax.fori_loop(0, rows // C, chunk, 0)

    @pl.when(t == pl.num_programs(1) - 1)
    def _():
        sfin_ref[0] = s_scr[...]


def _gla_prompt(z, a2, ab, gn, batch, seq):
    rows = _row_tile(seq, 512)
    nt = seq // rows
    zb = lambda off, w: pl.BlockSpec((rows, w), lambda b, t: (b * nt + t, off // w))
    full = lambda a: pl.BlockSpec(a.shape, lambda b, t: (0,) * a.ndim)
    return pl.pallas_call(
        functools.partial(_gla_prompt_kernel, rows=rows),
        grid=(batch, nt),
        in_specs=[zb(Z_GQK, 2 * GLA_QK_W), zb(Z_GV, GLA_V_W), zb(Z_GR, GLA_V_W), zb(Z_MISC, LANES),
                  full(a2), full(ab), full(gn)],
        out_specs=[pl.BlockSpec((rows, GLA_V_W), lambda b, t: (b * nt + t, 0)),
                   pl.BlockSpec((1, GLA_HEADS, GLA_DK, GLA_DV), lambda b, t: (b, 0, 0, 0))],
        out_shape=[jax.ShapeDtypeStruct((batch * seq, GLA_V_W), MXU_DTYPE),
                   jax.ShapeDtypeStruct((batch, GLA_HEADS, GLA_DK, GLA_DV), jnp.float32)],
        scratch_shapes=[pltpu.VMEM((GLA_HEADS, GLA_DK, GLA_DV), jnp.float32)],
        compiler_params=_cparams(("parallel", "arbitrary")),
        name="gla_prompt",
    )(z, z, z, z, a2, ab, gn)


def _gla_sample_kernel(qk_ref, v_ref, gr_ref, misc_ref, a2_ref, ab_ref, gn_ref, s0_ref, o_ref, s1_ref):
    hi = lax.Precision.HIGHEST
    eye = (lax.broadcasted_iota(jnp.int32, (GLA_DK, GLA_DK), 0)
           == lax.broadcasted_iota(jnp.int32, (GLA_DK, GLA_DK), 1))

    def col(row):
        return jnp.sum(jnp.where(eye, row, 0.0), axis=1, keepdims=True)

    glr = misc_ref[0][:, MISC_GLR:MISC_GLR + GLA_GATE_RANK]
    qk, v, gr = qk_ref[0], v_ref[0], gr_ref[0]
    for h in range(GLA_HEADS):
        ks = slice(h * GLA_DK, (h + 1) * GLA_DK)
        vs = slice(h * GLA_DV, (h + 1) * GLA_DV)
        q = qk[:, ks] * (GLA_DK ** -0.5)
        k = qk[:, GLA_QK_W + h * GLA_DK:GLA_QK_W + (h + 1) * GLA_DK]
        pre = jnp.dot(glr, a2_ref[:, ks], precision=hi, preferred_element_type=jnp.float32) + ab_ref[:, ks]
        b = _log_sigmoid(pre) / GLA_TAU
        s1 = col(jnp.exp(b)) * s0_ref[0, h] + col(k) * v[:, vs]
        s1_ref[0, h] = s1
        o = jnp.sum(col(q) * s1, axis=0, keepdims=True)
        ms = jnp.mean(o * o, axis=-1, keepdims=True)
        on = o * lax.rsqrt(ms + EPS) * gn_ref[...]
        g = gr[:, vs]
        o_ref[0, :, vs] = (on * (g * jax.nn.sigmoid(g))).astype(o_ref.dtype)


def _gla_sample(z, a2, ab, gn, s0):
    n = z.shape[0]
    z3 = z.reshape(n, 1, Z_W)
    zb = lambda off, w: pl.BlockSpec((1, 1, w), lambda b: (b, 0, off // w))
    full = lambda a: pl.BlockSpec(a.shape, lambda b: (0,) * a.ndim)
    sb = pl.BlockSpec((1, GLA_HEADS, GLA_DK, GLA_DV), lambda b: (b, 0, 0, 0))
    o, s1 = pl.pallas_call(
        _gla_sample_kernel,
        grid=(n,),
        in_specs=[zb(Z_GQK, 2 * GLA_QK_W), zb(Z_GV, GLA_V_W), zb(Z_GR, GLA_V_W), zb(Z_MISC, LANES),
                  full(a2), full(ab), full(gn), sb],
        out_specs=[pl.BlockSpec((1, 1, GLA_V_W), lambda b: (b, 0, 0)), sb],
        out_shape=[jax.ShapeDtypeStruct((n, 1, GLA_V_W), MXU_DTYPE),
                   jax.ShapeDtypeStruct(s0.shape, jnp.float32)],
        compiler_params=_cparams(("parallel",)),
        name="gla_sample",
    )(z3, z3, z3, z3, a2, ab, gn, s0)
    return o.reshape(n, GLA_V_W), s1


def _sortable(x):
    x = jnp.where(x == 0.0, 0.0, x)
    bits = pltpu.bitcast(x, jnp.int32)
    return jnp.where(bits < 0, bits ^ jnp.int32(0x7FFFFFFF), bits)


def _idx_select_kernel(iq_ref, misc_ref, ik_ref, mask_ref, key_scr, *, tq, rg, ck, seq, topk):
    i32 = jnp.int32
    ng = tq // rg
    nsub = ck // LANES
    nblk = seq // LANES
    qi = pl.program_id(1)
    nck = ((qi + 1) * tq + ck - 1) // ck
    lane = lax.broadcasted_iota(i32, (rg, LANES), 1)

    def slab(g, c, u):
        return key_scr.at[g * nblk + c * nsub + u]

    sg = min(tq, 2 * rg)
    for g in range(tq // sg):
        rows = slice(g * sg, (g + 1) * sg)
        row = qi * tq + g * sg + lax.broadcasted_iota(i32, (sg, 1), 0)
        w = misc_ref[rows, MISC_IW:MISC_IW + IDX_HEADS]
        iq = iq_ref[rows, :]
        iqh = [iq[:, h * IDX_DH:(h + 1) * IDX_DH] for h in range(IDX_HEADS)]
        wh = [w[:, h:h + 1] for h in range(IDX_HEADS)]

        def score_chunk(c, carry, g=g, row=row, iqh=iqh, wh=wh):
            c0 = pl.multiple_of(c * ck, ck)
            ks = ik_ref[pl.ds(c0, ck), :]
            acc = jnp.zeros((sg, ck), jnp.float32)
            for h in range(IDX_HEADS):
                s = lax.dot_general(iqh[h], ks, (((1,), (1,)), ((), ())), preferred_element_type=jnp.float32)
                acc = acc + wh[h] * jnp.maximum(s, 0.0)
            col = c0 + lax.broadcasted_iota(i32, (1, ck), 1)
            key = jnp.where(col <= row, _sortable(acc), INT_MIN)
            for k in range(sg // rg):
                for u in range(nsub):
                    slab(g * (sg // rg) + k, c, u)[...] = key[k * rg:(k + 1) * rg, u * LANES:(u + 1) * LANES]
            return carry

        lax.fori_loop(0, nck, score_chunk, 0)

    ones_mx = jnp.ones((LANES, LANES), MXU_DTYPE)

    def count(pred):
        accs = []
        for g in range(ng):
            def body(c, acc, g=g):
                c0 = pl.multiple_of(c * ck, ck)
                for u in range(nsub):
                    acc = acc + pred(g, slab(g, c, u)[...], lane + (c0 + u * LANES))
                return acc
            accs.append(lax.fori_loop(0, nck, body, jnp.zeros((rg, LANES), i32)))
        return [jnp.dot(a.astype(jnp.float32).astype(MXU_DTYPE), ones_mx, preferred_element_type=jnp.float32)
                for a in accs]

    def bit_step(i, ans):
        bit = lax.shift_left(i32(1), i32(31) - i)
        cand = [a ^ bit for a in ans]
        cnt = count(lambda g, blk, col: jnp.where(blk >= cand[g], 1, 0))
        return tuple(jnp.where(cnt[g] >= topk, cand[g], ans[g]) for g in range(ng))

    ans = lax.fori_loop(0, 32, bit_step, tuple(jnp.full((rg, LANES), INT_MIN, i32) for _ in range(ng)))
    tb = [jnp.maximum(a, INT_MIN + 1) for a in ans]

    c_gt = count(lambda g, blk, col: jnp.where(blk > tb[g], 1, 0))
    c_eq = count(lambda g, blk, col: jnp.where(blk == tb[g], 1, 0))
    need = [topk - c for c in c_gt]
    excess = [c_eq[g] > need[g] for g in range(ng)]
    n_excess = sum(jnp.max(jnp.where(e, 1, 0)) for e in excess)

    def write(select):
        def write_chunk(c, carry):
            c0 = pl.multiple_of(c * ck, ck)
            for g in range(ng):
                for u in range(nsub):
                    sel = select(g, slab(g, c, u)[...], lane + (c0 + u * LANES))
                    mask_ref[g * rg:(g + 1) * rg, pl.ds(c0 + u * LANES, LANES)] = sel.astype(mask_ref.dtype)
            return carry
        lax.fori_loop(0, nck, write_chunk, 0)

    @pl.when(n_excess == 0)
    def _():
        write(lambda g, blk, col: jnp.where(blk >= tb[g], 1, 0))

    @pl.when(n_excess > 0)
    def _():
        nbits = max(1, (seq - 1).bit_length())

        def jbit(i, jv):
            bit = lax.shift_left(i32(1), i32(nbits - 1) - i)
            cand = [j | bit for j in jv]
            gcnt = count(lambda g, blk, col: jnp.where(blk == tb[g], jnp.where(col < cand[g], 1, 0), 0))
            return tuple(jnp.where(gcnt[g] < need[g], cand[g], jv[g]) for g in range(ng))

        jv = lax.fori_loop(0, nbits, jbit, tuple(jnp.zeros((rg, LANES), i32) for _ in range(ng)))
        jb = [jnp.where(excess[g], jv[g], seq) for g in range(ng)]
        write(lambda g, blk, col: jnp.where(blk > tb[g], 1,
                                            jnp.where(blk == tb[g], jnp.where(col <= jb[g], 1, 0), 0)))

    zeros8 = jnp.zeros((tq, ck), mask_ref.dtype)

    def zero_chunk(c, carry):
        mask_ref[:, pl.ds(pl.multiple_of(c * ck, ck), ck)] = zeros8
        return carry

    lax.fori_loop(nck, seq // ck, zero_chunk, 0)


def _idx_select(iqb, misc2, ikb, batch, seq, topk):
    tq = _row_tile(seq, 512)
    rg = _row_tile(tq, 128)
    ck = _row_tile(seq, 512)
    nq = seq // tq
    return pl.pallas_call(
        functools.partial(_idx_select_kernel, tq=tq, rg=rg, ck=ck, seq=seq, topk=topk),
        grid=(batch, nq),
        in_specs=[pl.BlockSpec((tq, IDX_Q_W), lambda b, q: (b * nq + q, 0)),
                  pl.BlockSpec((tq, LANES), lambda b, q: (b * nq + q, 0)),
                  pl.BlockSpec((seq, IDX_DH), lambda b, q: (b, 0))],
        out_specs=pl.BlockSpec((tq, seq), lambda b, q: (b * nq + q, 0)),
        out_shape=jax.ShapeDtypeStruct((batch * seq, seq), jnp.int8),
        scratch_shapes=[pltpu.VMEM(((tq // rg) * (seq // LANES), rg, LANES), jnp.int32)],
        compiler_params=_cparams(("parallel", "arbitrary")),
        name="idx_select",
    )(iqb, misc2, ikb)


def _attn_kernel(q_ref, k_ref, v_ref, m_ref, o_ref, qb_scr, acc_scr, m_scr, l_scr):
    qi, ki = pl.program_id(1), pl.program_id(2)
    t, tk = q_ref.shape[0], k_ref.shape[0]
    nsub = tk // LANES

    @pl.when(ki == 0)
    def _():
        qb_scr[...] = (q_ref[...] * (ATT_DH ** -0.5 * math.log2(math.e))).astype(qb_scr.dtype)
        acc_scr[...] = jnp.zeros_like(acc_scr)
        m_scr[...] = jnp.full(m_scr.shape, NEG, jnp.float32)
        l_scr[...] = jnp.zeros_like(l_scr)

    @pl.when(ki <= qi)
    def _():
        bias = jnp.where(m_ref[...].astype(jnp.int32) != 0, 0.0, NEG)
        ones = jnp.ones((tk, LANES), v_ref.dtype)
        for h in range(ATT_HEADS):
            sl = slice(h * ATT_DH, (h + 1) * ATT_DH)
            s = lax.dot_general(qb_scr[:, sl], k_ref[:, sl], (((1,), (1,)), ((), ())),
                                preferred_element_type=jnp.float32) + bias
            mx = s[:, 0:LANES]
            for u in range(1, nsub):
                mx = jnp.maximum(mx, s[:, u * LANES:(u + 1) * LANES])
            m_prev = m_scr[h]
            m_new = jnp.maximum(m_prev, jnp.max(mx, axis=1, keepdims=True))
            alpha = jnp.exp2(m_prev - m_new)
            p = jnp.exp2(s - jnp.concatenate([m_new] * nsub, axis=1)).astype(v_ref.dtype)
            pv = jnp.dot(p, jnp.concatenate([v_ref[:, sl], ones], axis=1), preferred_element_type=jnp.float32)
            m_scr[h] = m_new
            l_scr[h] = alpha * l_scr[h] + pv[:, ATT_DH:]
            acc_scr[:, sl] = alpha * acc_scr[:, sl] + pv[:, :ATT_DH]

    @pl.when(ki == qi)
    def _():
        for h in range(ATT_HEADS):
            sl = slice(h * ATT_DH, (h + 1) * ATT_DH)
            o_ref[:, sl] = (acc_scr[:, sl] / l_scr[h]).astype(o_ref.dtype)


def _attn(q, kb, vb, mask, batch, seq):
    t = _row_tile(seq, 512)
    nt = seq // t
    return pl.pallas_call(
        _attn_kernel,
        grid=(batch, nt, nt),
        in_specs=[pl.BlockSpec((t, ATT_W), lambda b, i, j: (b * nt + i, 0)),
                  pl.BlockSpec((t, ATT_W), lambda b, i, j: (b * nt + jnp.minimum(i, j), 0)),
                  pl.BlockSpec((t, ATT_W), lambda b, i, j: (b * nt + jnp.minimum(i, j), 0)),
                  pl.BlockSpec((t, t), lambda b, i, j: (b * nt + i, jnp.minimum(i, j)))],
        out_specs=pl.BlockSpec((t, ATT_W), lambda b, i, j: (b * nt + i, 0)),
        out_shape=jax.ShapeDtypeStruct((batch * seq, ATT_W), MXU_DTYPE),
        scratch_shapes=[pltpu.VMEM((t, ATT_W), MXU_DTYPE), pltpu.VMEM((t, ATT_W), jnp.float32),
                        pltpu.VMEM((ATT_HEADS, t, LANES), jnp.float32),
                        pltpu.VMEM((ATT_HEADS, t, LANES), jnp.float32)],
        compiler_params=_cparams(("parallel", "parallel", "arbitrary")),
        name="attn",
    )(q, kb, vb, mask)


def _sidx_kernel(pt_ref, iq_ref, w_ref, ikn_ref, cache_ref, out_ref, buf, sem, *, n_pages):
    b = pl.program_id(0)
    nb = pl.num_programs(0)
    past = n_pages * PAGE_SIZE

    def page_copy(bb, slot, p):
        return pltpu.make_async_copy(cache_ref.at[pt_ref[bb, p]],
                                     buf.at[slot, :, pl.ds(p * PAGE_SIZE, PAGE_SIZE)], sem.at[slot])

    def start(bb, slot):
        for p in range(n_pages):
            page_copy(bb, slot, p).start()

    @pl.when(b == 0)
    def _():
        start(0, 0)

    @pl.when(b + 1 < nb)
    def _():
        start(b + 1, (b + 1) % 2)

    slot = b % 2
    for p in range(n_pages):
        page_copy(b, slot, p).wait()

    iq = iq_ref[0]
    w = w_ref[0]
    ikt = buf[slot].astype(iq.dtype)
    s = jnp.dot(iq, ikt, preferred_element_type=jnp.float32)
    sc = jnp.sum(w * jnp.maximum(s, 0.0), axis=0, keepdims=True)
    out_ref[0, :, 0:past] = jnp.where(sc == 0.0, 0.0, sc)
    s_new = jnp.sum(iq.astype(jnp.float32) * ikn_ref[0].astype(iq.dtype).astype(jnp.float32), axis=1, keepdims=True)
    sc_new = jnp.sum(w * jnp.maximum(s_new, 0.0), axis=0, keepdims=True)
    sc_new = jnp.where(sc_new == 0.0, 0.0, sc_new)
    lane = lax.broadcasted_iota(jnp.int32, (1, LANES), 1)
    out_ref[0, :, past:past + LANES] = jnp.where(lane == 0, sc_new, -jnp.inf)


def _sidx(page_table, iq3, w3, ikn3, cache_idx):
    bd, n_pages = page_table.shape
    past = n_pages * PAGE_SIZE
    grid_spec = pltpu.PrefetchScalarGridSpec(
        num_scalar_prefetch=1,
        grid=(bd,),
        in_specs=[pl.BlockSpec((1, IDX_HEADS, IDX_DH), lambda b, pt: (b, 0, 0)),
                  pl.BlockSpec((1, IDX_HEADS, 1), lambda b, pt: (b, 0, 0)),
                  pl.BlockSpec((1, 1, IDX_DH), lambda b, pt: (b, 0, 0)),
                  pl.BlockSpec(memory_space=pl.ANY)],
        out_specs=pl.BlockSpec((1, 1, past + LANES), lambda b, pt: (b, 0, 0)),
        scratch_shapes=[pltpu.VMEM((2, IDX_DH, past), jnp.float32), pltpu.SemaphoreType.DMA((2,))],
    )
    return pl.pallas_call(
        functools.partial(_sidx_kernel, n_pages=n_pages),
        grid_spec=grid_spec,
        out_shape=jax.ShapeDtypeStruct((bd, 1, past + LANES), jnp.float32),
        compiler_params=_cparams(("arbitrary",)),
        name="sidx",
    )(page_table, iq3, w3, ikn3, cache_idx)


def _stopk_kernel(sc_ref, sel_ref, x_scr, *, topk):
    x_scr[...] = sc_ref[...]
    shape = x_scr.shape
    big = jnp.int32(2 ** 30)

    def body(it, sel):
        x = x_scr[...]
        col = lax.broadcasted_iota(jnp.int32, shape, 1)
        m = jnp.max(x, axis=1, keepdims=True)
        idx = jnp.min(jnp.where(x == m, col, big), axis=1, keepdims=True)
        x_scr[...] = jnp.where(col == idx, -jnp.inf, x)
        kcol = lax.broadcasted_iota(jnp.int32, sel.shape, 1)
        return jnp.where(kcol == it, idx, sel)

    sel_ref[...] = lax.fori_loop(0, topk, body, jnp.zeros(sel_ref.shape, jnp.int32))


def _stopk(scores, topk):
    bd, lp = scores.shape
    return pl.pallas_call(
        functools.partial(_stopk_kernel, topk=topk),
        out_shape=jax.ShapeDtypeStruct((bd, topk), jnp.int32),
        scratch_shapes=[pltpu.VMEM((bd, lp), jnp.float32)],
        compiler_params=pltpu.CompilerParams(vmem_limit_bytes=VMEM_LIMIT),
        name="stopk",
    )(scores)


def _sattn_kernel(sel_ref, pt_ref, q_ref, kn_ref, vn_ref, selv_ref, ck_ref, cv_ref, o_ref, kbuf, vbuf, sem,
                  *, topk, past):
    b = pl.program_id(0)
    nb = pl.num_programs(0)

    def row_copies(bb, slot, j):
        p = jnp.minimum(sel_ref[bb, j], past - 1)
        phys = pt_ref[bb, p // PAGE_SIZE]
        r = p % PAGE_SIZE
        return (pltpu.make_async_copy(ck_ref.at[phys, r], kbuf.at[slot, j], sem.at[0, slot]),
                pltpu.make_async_copy(cv_ref.at[phys, r], vbuf.at[slot, j], sem.at[1, slot]))

    def start(bb, slot):
        def body(j, c):
            ck, cv = row_copies(bb, slot, j)
            ck.start()
            cv.start()
            return c
        lax.fori_loop(0, topk, body, 0)

    @pl.when(b == 0)
    def _():
        start(0, 0)

    @pl.when(b + 1 < nb)
    def _():
        start(b + 1, (b + 1) % 2)

    slot = b % 2

    def wbody(j, c):
        ck, cv = row_copies(b, slot, j)
        ck.wait()
        cv.wait()
        return c

    lax.fori_loop(0, topk, wbody, 0)

    selv = selv_ref[0]
    in_past = selv < past
    q = q_ref[...]
    kg = jnp.where(in_past, kbuf[slot], kn_ref[...])
    vg = jnp.where(in_past, vbuf[slot], vn_ref[...])
    s = jnp.sum(kg * q, axis=-1, keepdims=True) * (ATT_DH ** -0.5)
    s = jnp.where(selv <= past, s, -jnp.inf)
    m = jnp.max(s, axis=0, keepdims=True)
    p = jnp.exp(s - m)
    p = p / jnp.sum(p, axis=0, keepdims=True)
    o_ref[...] = jnp.sum(p * vg, axis=0, keepdims=True).astype(o_ref.dtype)


def _sattn(sel, page_table, q3, kn3, vn3, cache_k, cache_v):
    bd, topk = sel.shape
    past = page_table.shape[1] * PAGE_SIZE
    hb = pl.BlockSpec((1, ATT_HEADS, ATT_DH), lambda b, s, pt: (b, 0, 0))
    grid_spec = pltpu.PrefetchScalarGridSpec(
        num_scalar_prefetch=2,
        grid=(bd,),
        in_specs=[hb, hb, hb, pl.BlockSpec((1, topk, 1, 1), lambda b, s, pt: (b, 0, 0, 0)),
                  pl.BlockSpec(memory_space=pl.ANY), pl.BlockSpec(memory_space=pl.ANY)],
        out_specs=hb,
        scratch_shapes=[pltpu.VMEM((2, topk, ATT_HEADS, ATT_DH), jnp.float32),
                        pltpu.VMEM((2, topk, ATT_HEADS, ATT_DH), jnp.float32),
                        pltpu.SemaphoreType.DMA((2, 2))],
    )
    return pl.pallas_call(
        functools.partial(_sattn_kernel, topk=topk, past=past),
        grid_spec=grid_spec,
        out_shape=jax.ShapeDtypeStruct((bd, ATT_HEADS, ATT_DH), MXU_DTYPE),
        compiler_params=_cparams(("arbitrary",)),
        name="sattn",
    )(sel, page_table, q3, kn3, vn3, sel.reshape(bd, topk, 1, 1), cache_k, cache_v)


def _merge_kernel(og_ref, oa_ref, ga_ref, gb_ref, x_ref, wg_ref, wa_ref, wo_ref, nf_ref, x1_ref, hn_ref):
    f32 = jnp.float32
    mix = (jax.nn.sigmoid(ga_ref[...]) * jnp.dot(og_ref[...], wg_ref[...], preferred_element_type=f32)
           + jax.nn.sigmoid(gb_ref[...]) * jnp.dot(oa_ref[...], wa_ref[...], preferred_element_type=f32))
    x1 = x_ref[...] + jnp.dot(mix.astype(wo_ref.dtype), wo_ref[...], preferred_element_type=f32)
    x1_ref[...] = x1
    ms = jnp.mean(x1 * x1, axis=-1, keepdims=True)
    hn_ref[...] = (x1 * lax.rsqrt(ms + EPS) * nf_ref[...]).astype(hn_ref.dtype)


def _merge(og, oa, z, x, wg, wa, wo, nf):
    n = x.shape[0]
    tm = _row_tile(n, 512)
    rb = lambda: pl.BlockSpec((tm, D_MODEL), lambda i: (i, 0))
    zb = lambda off: pl.BlockSpec((tm, D_MODEL), lambda i: (i, off // D_MODEL))
    wb = lambda: pl.BlockSpec((D_MODEL, D_MODEL), lambda i: (0, 0))
    return pl.pallas_call(
        _merge_kernel,
        grid=(n // tm,),
        in_specs=[rb(), rb(), zb(Z_GA), zb(Z_GB), rb(), wb(), wb(), wb(), pl.BlockSpec((1, D_MODEL), lambda i: (0, 0))],
        out_specs=[rb(), rb()],
        out_shape=[jax.ShapeDtypeStruct((n, D_MODEL), jnp.float32), jax.ShapeDtypeStruct((n, D_MODEL), MXU_DTYPE)],
        compiler_params=_cparams(("parallel",)),
        name="merge",
    )(og, oa, z, z, x, wg, wa, wo, nf)


_PAIRS = [(r1, r2) for r1 in range(PEER_TOPK) for r2 in range(PEER_TOPK) if (r1 + 1) * (r2 + 1) <= PEER_TOPK]


def _top16_sorted(x, exact_ties):
    nk = x.shape[0]
    rid = lax.broadcasted_iota(jnp.int32, x.shape, 0)
    rank = jnp.full(x.shape, PEER_TOPK, jnp.int32)
    vals = []
    for r in range(PEER_TOPK):
        m = jnp.max(x, axis=0, keepdims=True)
        hit = x == m
        if exact_ties:
            hit = rid == jnp.min(jnp.where(hit, rid, nk), axis=0, keepdims=True)
        x = jnp.where(hit, -jnp.inf, x)
        rank = jnp.where(hit, r, rank)
        vals.append(m)
    return vals, rank


def _router_body(hn_ref, wqt_ref, keys_ref, r2_ref, e2_ref, cnt_ref, e1_ref, exact_ties):
    f32 = jnp.float32
    qt = lax.dot_general(wqt_ref[...], hn_ref[...], (((1,), (1,)), ((), ())), preferred_element_type=f32)
    qt = qt.astype(MXU_DTYPE)
    half = PEER_DKEY // 2
    npair = len(_PAIRS)
    ok = jnp.zeros((1, hn_ref.shape[0]), jnp.int32)
    for h in range(PEER_HEADS):
        s1 = jnp.dot(keys_ref[2 * h], qt[(2 * h) * half:(2 * h + 1) * half, :], preferred_element_type=f32)
        s2 = jnp.dot(keys_ref[2 * h + 1], qt[(2 * h + 1) * half:(2 * h + 2) * half, :], preferred_element_type=f32)
        v1, rank1 = _top16_sorted(s1, exact_ties)
        v2, rank2 = _top16_sorted(s2, exact_ties)
        sel1 = rank1 < PEER_TOPK
        sel2 = rank2 < PEER_TOPK
        n12 = (jnp.sum(jnp.where(sel1, 1, 0), axis=0, keepdims=True)
               + jnp.sum(jnp.where(sel2, 1, 0), axis=0, keepdims=True))
        ok = jnp.where(n12 == 2 * PEER_TOPK, ok, -1)
        cand = jnp.concatenate([v1[a] + v2[b] for a, b in _PAIRS], axis=0)
        pid = lax.broadcasted_iota(jnp.int32, cand.shape, 0)
        c = cand
        picked = jnp.zeros(cand.shape, jnp.int32)
        for _ in range(PEER_TOPK):
            m = jnp.max(c, axis=0, keepdims=True)
            hit = pid == jnp.min(jnp.where(c == m, pid, npair), axis=0, keepdims=True)
            c = jnp.where(hit, -jnp.inf, c)
            picked = jnp.where(hit, 1, picked)
        top = v1[0] + v2[0]
        zsum = jnp.sum(jnp.where(picked > 0, jnp.exp(cand - top), 0.0), axis=0, keepdims=True)
        half_inv_z = 0.5 / zsum
        cnt = jnp.zeros(s1.shape, f32)
        for a in range(PEER_TOPK):
            rows = [i for i, (pa, _) in enumerate(_PAIRS) if pa == a]
            n_a = jnp.sum(picked[rows[0]:rows[-1] + 1], axis=0, keepdims=True).astype(f32)
            cnt = jnp.where(rank1 == a, n_a, cnt)
        cnt_ref[0, h] = cnt
        e1_ref[0, h] = jnp.where(sel1, jnp.exp(s1 - v1[0]) * half_inv_z, 0.0)
        r2_ref[0, h] = rank2.astype(f32).astype(r2_ref.dtype)
        e2_ref[0, h] = jnp.where(sel2, jnp.exp(s2 - v2[0]), 0.0).astype(e2_ref.dtype)
    return ok


def _router_kernel(hn_ref, wqt_ref, keys_ref, r2_ref, e2_ref, cnt_ref, e1_ref):
    outs = (r2_ref, e2_ref, cnt_ref, e1_ref)
    ok = _router_body(hn_ref, wqt_ref, keys_ref, *outs, exact_ties=False)

    @pl.when(jnp.min(ok) < 0)
    def _():
        _router_body(hn_ref, wqt_ref, keys_ref, *outs, exact_ties=True)


def _router(hn, wqt, keys, tm):
    n = hn.shape[0]
    nt = n // tm
    tr = min(tm, LANES)
    per = tm // tr
    shp = lambda dt: jax.ShapeDtypeStruct((nt, PEER_HEADS, PEER_NKEYS, tm), dt)
    ob = pl.BlockSpec((1, PEER_HEADS, PEER_NKEYS, tr), lambda i: (i // per, 0, 0, i % per))
    return pl.pallas_call(
        _router_kernel,
        grid=(n // tr,),
        in_specs=[pl.BlockSpec((tr, D_MODEL), lambda i: (i, 0)),
                  pl.BlockSpec((D_MODEL, D_MODEL), lambda i: (0, 0)),
                  pl.BlockSpec(keys.shape, lambda i: (0, 0, 0))],
        out_specs=[ob, ob, ob, ob],
        out_shape=[shp(GATE_DTYPE), shp(GATE_DTYPE), shp(jnp.float32), shp(jnp.float32)],
        compiler_params=_cparams(("parallel",)),
        name="router",
    )(hn, wqt, keys)


def _gelu2(x):
    return x * (1.0 + lax.erf(x * (2.0 ** -0.5)))


def _peer_kernel(hn_ref, u_ref, vt_ref, r2_ref, e2_ref, cnt_ref, e1_ref, x1_ref, y_ref, acc_scr, p_scr, *, rows1):
    j = pl.program_id(1)
    f32 = jnp.float32
    gd = r2_ref.dtype
    tm = hn_ref.shape[0]
    sub = 16
    grp = PEER_NKEYS // sub

    @pl.when(j == 0)
    def _():
        acc_scr[...] = jnp.zeros_like(acc_scr)

    at = lax.dot_general(u_ref[...], hn_ref[...], (((1,), (1,)), ((), ())), preferred_element_type=f32)
    for c in range(rows1):
        w = jnp.zeros((grp, sub, tm), gd)
        for h in range(PEER_HEADS):
            cnt = jnp.broadcast_to(cnt_ref[0, h, c:c + 1, :], (sub, tm)).astype(gd)[None]
            e1 = jnp.broadcast_to(e1_ref[0, h, c:c + 1, :], (sub, tm)).astype(gd)[None]
            r2 = r2_ref[0, h].reshape(grp, sub, tm)
            e2 = e2_ref[0, h].reshape(grp, sub, tm)
            w = w + jnp.where(r2 < cnt, e2, jnp.zeros_like(e2)) * e1
        rs = slice(c * PEER_NKEYS, (c + 1) * PEER_NKEYS)
        p_scr[rs, :] = w.reshape(PEER_NKEYS, tm) * _gelu2(at[rs, :]).astype(gd)
    acc_scr[...] += jnp.dot(vt_ref[...], p_scr[...], preferred_element_type=f32)

    @pl.when(j == pl.num_programs(1) - 1)
    def _():
        y_ref[...] = x1_ref[...] + acc_scr[...].T


def _peer(hn, u, vt, router_out, x1, tm):
    n = hn.shape[0]
    nt = n // tm
    rows1 = 16
    te = rows1 * PEER_NKEYS
    ne = (PEER_NKEYS * PEER_NKEYS) // te
    r2, e2, cnt, e1 = router_out
    full = pl.BlockSpec((1, PEER_HEADS, PEER_NKEYS, tm), lambda i, j: (i, 0, 0, 0))
    part = pl.BlockSpec((1, PEER_HEADS, rows1, tm), lambda i, j: (i, 0, j, 0))
    return pl.pallas_call(
        functools.partial(_peer_kernel, rows1=rows1),
        grid=(nt, ne),
        in_specs=[pl.BlockSpec((tm, D_MODEL), lambda i, j: (i, 0)),
                  pl.BlockSpec((te, D_MODEL), lambda i, j: (j, 0)),
                  pl.BlockSpec((D_MODEL, te), lambda i, j: (0, j)),
                  full, full, part, part,
                  pl.BlockSpec((tm, D_MODEL), lambda i, j: (i, 0))],
        out_specs=pl.BlockSpec((tm, D_MODEL), lambda i, j: (i, 0)),
        out_shape=jax.ShapeDtypeStruct((n, D_MODEL), jnp.float32),
        scratch_shapes=[pltpu.VMEM((D_MODEL, tm), jnp.float32), pltpu.VMEM((te, tm), MXU_DTYPE)],
        compiler_params=_cparams(("parallel", "arbitrary")),
        name="peer",
    )(hn, u, vt, r2, e2, cnt, e1, x1)


def _transpose_cast_kernel(x_ref, o_ref):
    o_ref[...] = x_ref[...].T.astype(o_ref.dtype)


def _transpose_cast(x, dtype):
    r, c = x.shape
    tr = _row_tile(r, 512)
    return pl.pallas_call(
        _transpose_cast_kernel,
        grid=(r // tr,),
        in_specs=[pl.BlockSpec((tr, c), lambda i: (i, 0))],
        out_specs=pl.BlockSpec((c, tr), lambda i: (0, i)),
        out_shape=jax.ShapeDtypeStruct((c, r), dtype),
        compiler_params=_cparams(("parallel",)),
        name="transpose_cast",
    )(x)


def _reorder_w_in(w_in):
    sizes = (GLA_QK_W, GLA_QK_W, GLA_V_W, GLA_V_W, GLA_GATE_RANK, ATT_W, ATT_W, ATT_W, IDX_Q_W, IDX_DH, IDX_HEADS,
             D_MODEL, D_MODEL)
    offs = [0]
    for s in sizes:
        offs.append(offs[-1] + s)
    gq, gk, gv, gr, glr, aq, ak, av, iq, ik, iw, ga, gb = (w_in[:, offs[i]:offs[i + 1]] for i in range(len(sizes)))
    pad = jnp.zeros((w_in.shape[0], LANES - IDX_DH - IDX_HEADS - GLA_GATE_RANK), w_in.dtype)
    return jnp.concatenate([gq, gk, gv, gr, aq, ak, av, ga, gb, iq, ik, iw, glr, pad], axis=1)


def _ffn(og, oa, z, x, wg, wa, wo, nf, wqt, keys, u, vt):
    n = x.shape[0]
    x1, hn = _merge(og, oa, z, x, wg, wa, wo, nf)
    tm = _row_tile(n, 512)
    r = _router(hn, wqt, keys, tm)
    return _peer(hn, u, vt, r, x1, tm)


def kernel(x_prompt, x_sample, cache_k, cache_v, cache_idx_k, state_gla, page_table, norm_mix, w_in, gla_a2,
           gla_a_bias, q_norm, k_norm, gla_norm, w_br_gla, w_br_att, w_out, norm_ffn, peer_wq, peer_keys,
           peer_u, peer_v):
    depth = w_in.shape[0]
    assert depth == 1 and x_sample.shape[1] == 1
    B, S, _ = x_prompt.shape
    Bd = x_sample.shape[0]
    n_pages = page_table.shape[1]
    past = n_pages * PAGE_SIZE
    bf = MXU_DTYPE
    l = 0

    w_all = _reorder_w_in(w_in[l]).astype(bf)
    g_mix = norm_mix[l].reshape(1, D_MODEL)
    a2, ab = gla_a2[l], gla_a_bias[l].reshape(1, GLA_QK_W)
    gn = gla_norm[l].reshape(1, GLA_DV)
    qn, kn = q_norm[l].reshape(1, ATT_DH), k_norm[l].reshape(1, ATT_DH)
    wg, wa, wo = w_br_gla[l].astype(bf), w_br_att[l].astype(bf), w_out[l].astype(bf)
    nf = norm_ffn[l].reshape(1, D_MODEL)
    wqt = _transpose_cast(peer_wq[l], bf)
    keys = peer_keys[l].reshape(PEER_HEADS * 2, PEER_NKEYS, PEER_DKEY // 2).astype(bf)
    u = peer_u[l].astype(bf)
    vt = _transpose_cast(peer_v[l], bf)
    ffn_w = (wg, wa, wo, nf, wqt, keys, u, vt)

    xp = x_prompt.reshape(B * S, D_MODEL)
    zp = _proj(xp, g_mix, w_all)
    tabs_p = _rope_tables(jnp.arange(S))
    qp, kp, kpb, vpb, iqpb, miscp, ikpb = _post(zp, tabs_p, S, qn, kn)
    ogp, sp = _gla_prompt(zp, a2, ab, gn, B, S)
    topk_p = min(TOPK_MAX, S // 4)
    maskp = _idx_select(iqpb, miscp, ikpb, B, S, topk_p)
    oap = _attn(qp, kpb, vpb, maskp, B, S)
    yp = _ffn(ogp, oap, zp, xp, *ffn_w)

    xs = x_sample.reshape(Bd, D_MODEL)
    zs = _proj(xs, g_mix, w_all)
    tabs_s = _rope_tables(jnp.full((Bd,), past, jnp.int32))
    qs, ks, _, _, iqsb, miscs, iksb = _post(zs, tabs_s, Bd, qn, kn)
    vs = zs[:, Z_AV:Z_AV + ATT_W]
    ogs, ss = _gla_sample(zs, a2, ab, gn, state_gla[l])
    scores = _sidx(page_table, iqsb.reshape(Bd, IDX_HEADS, IDX_DH),
                   miscs[:, MISC_IW:MISC_IW + IDX_HEADS].reshape(Bd, IDX_HEADS, 1),
                   iksb.reshape(Bd, 1, IDX_DH), jnp.swapaxes(cache_idx_k[l], 1, 2))
    topk_s = min(TOPK_MAX, (past + 1) // 4)
    sel = _stopk(scores.reshape(Bd, past + LANES), topk_s)
    hd = (Bd, ATT_HEADS, ATT_DH)
    oas = _sattn(sel, page_table, qs.reshape(hd), ks.reshape(hd), vs.reshape(hd), cache_k[l], cache_v[l])
    ys = _ffn(ogs, oas.reshape(Bd, ATT_W), zs, xs, *ffn_w)

    return (yp.reshape(B, S, D_MODEL), ys.reshape(Bd, 1, D_MODEL),
            kp.reshape(1, B, S, ATT_HEADS, ATT_DH), zp[:, Z_AV:Z_AV + ATT_W].reshape(1, B, S, ATT_HEADS, ATT_DH),
            miscp[:, :IDX_DH].reshape(1, B, S, IDX_DH), sp[None],
            ks.reshape(1, Bd, 1, ATT_HEADS, ATT_DH), vs.reshape(1, Bd, 1, ATT_HEADS, ATT_DH),
            miscs[:, :IDX_DH].reshape(1, Bd, 1, IDX_DH), ss[None])
```

```python
import functools
import math

import jax
import jax.numpy as jnp
from jax import lax
from jax.experimental import pallas as pl
from jax.experimental.pallas import tpu as pltpu

D_MODEL = 1024
GLA_HEADS, GLA_DK, GLA_DV = 4, 128, 256
GLA_GATE_RANK = 16
GLA_TAU = 16.0
GLA_CHUNK = 64
ATT_HEADS, ATT_DH = 8, 128
IDX_HEADS, IDX_DH = 8, 64
TOPK_MAX = 256
PAGE_SIZE = 128
ROPE_THETA = 10000.0
PEER_HEADS, PEER_NKEYS, PEER_DKEY, PEER_TOPK = 8, 128, 128, 16
EPS = 1e-6

GLA_QK_W = GLA_HEADS * GLA_DK
GLA_V_W = GLA_HEADS * GLA_DV
ATT_W = ATT_HEADS * ATT_DH
IDX_Q_W = IDX_HEADS * IDX_DH

LANES = 128
VMEM_LIMIT = 56 * 1024 * 1024
MXU_DTYPE = jnp.bfloat16
GATE_DTYPE = jnp.bfloat16
NEG = -1e30
INT_MIN = -(2 ** 31)

Z_GQK, Z_GV, Z_GR, Z_AQ, Z_AK, Z_AV, Z_GA, Z_GB = (i * 1024 for i in range(8))
Z_IQ = 8 * 1024
Z_MISC = Z_IQ + IDX_Q_W
Z_W = Z_MISC + LANES
MISC_IK, MISC_IW, MISC_GLR = 0, IDX_DH, IDX_DH + IDX_HEADS


def _cparams(sem):
    return pltpu.CompilerParams(dimension_semantics=sem, vmem_limit_bytes=VMEM_LIMIT)


def _row_tile(n, pref):
    t = min(pref, n)
    while n % t:
        t //= 2
    return t


def _proj_kernel(x_ref, g_ref, w_ref, z_ref, h_scr):
    @pl.when(pl.program_id(1) == 0)
    def _():
        x = x_ref[...]
        ms = jnp.mean(x * x, axis=-1, keepdims=True)
        h_scr[...] = (x * lax.rsqrt(ms + EPS) * g_ref[...]).astype(h_scr.dtype)

    z_ref[...] = jnp.dot(h_scr[...], w_ref[...], preferred_element_type=jnp.float32)


def _proj(x, g, w):
    n = x.shape[0]
    tm = _row_tile(n, 512)
    nblk = 3
    tn = Z_W // nblk
    return pl.pallas_call(
        _proj_kernel,
        grid=(n // tm, nblk),
        in_specs=[pl.BlockSpec((tm, D_MODEL), lambda i, j: (i, 0)),
                  pl.BlockSpec((1, D_MODEL), lambda i, j: (0, 0)),
                  pl.BlockSpec((D_MODEL, tn), lambda i, j: (0, j))],
        out_specs=pl.BlockSpec((tm, tn), lambda i, j: (i, j)),
        out_shape=jax.ShapeDtypeStruct((n, Z_W), jnp.float32),
        scratch_shapes=[pltpu.VMEM((tm, D_MODEL), MXU_DTYPE)],
        compiler_params=_cparams(("parallel", "arbitrary")),
        name="proj",
    )(x, g, w)


def _post_kernel(aq_ref, ak_ref, av_ref, iq_ref, misc_ref, c128_ref, s128_ref, c64_ref, s64_ref,
                 qn_ref, kn_ref, q_out, k_out, kb_out, vb_out, iqb_out, misc_out, ikb_out):
    c128, s128 = c128_ref[...], s128_ref[...]

    def norm_rope(x, g):
        ms = jnp.mean(x * x, axis=-1, keepdims=True)
        y = x * lax.rsqrt(ms + EPS) * g
        return y * c128 + pltpu.roll(y, ATT_DH // 2, 1) * s128

    for h in range(ATT_HEADS):
        sl = slice(h * ATT_DH, (h + 1) * ATT_DH)
        q_out[:, sl] = norm_rope(aq_ref[:, sl], qn_ref[...])
        kk = norm_rope(ak_ref[:, sl], kn_ref[...])
        k_out[:, sl] = kk
        kb_out[:, sl] = kk.astype(kb_out.dtype)
    vb_out[...] = av_ref[...].astype(vb_out.dtype)

    half = IDX_DH // 2
    c64, s64 = c64_ref[...], s64_ref[...]

    def rope64(x, reps):
        w = x.shape[1]
        lane = lax.broadcasted_iota(jnp.int32, x.shape, 1)
        first = (lane % IDX_DH) < half
        partner = jnp.where(first, pltpu.roll(x, w - half, 1), pltpu.roll(x, half, 1))
        c = jnp.concatenate([c64] * reps, axis=1) if reps > 1 else c64
        s = jnp.concatenate([s64] * reps, axis=1) if reps > 1 else s64
        return x * c + partner * s

    iq = rope64(iq_ref[...], IDX_Q_W // LANES) * (IDX_DH ** -0.5)
    iqb_out[...] = iq.astype(iqb_out.dtype)
    misc = misc_ref[...]
    rot = rope64(misc, 1)
    lane = lax.broadcasted_iota(jnp.int32, misc.shape, 1)
    out = jnp.where(lane < MISC_IW, rot,
                    jnp.where(lane < MISC_GLR, misc * (IDX_HEADS ** -0.5), misc))
    misc_out[...] = out
    ikb_out[...] = rot[:, :IDX_DH].astype(ikb_out.dtype)


def _post(z, tabs, tab_rows, q_norm, k_norm):
    n = z.shape[0]
    tm = _row_tile(min(n, tab_rows), 512)
    ntab = tab_rows // tm
    c128, s128, c64, s64 = tabs
    zb = lambda off, w: pl.BlockSpec((tm, w), lambda i: (i, off // w))
    tb = pl.BlockSpec((tm, LANES), lambda i: (i % ntab, 0))
    gb = pl.BlockSpec((1, ATT_DH), lambda i: (0, 0))
    ob = lambda w: pl.BlockSpec((tm, w), lambda i: (i, 0))
    f32, bf = jnp.float32, MXU_DTYPE
    return pl.pallas_call(
        _post_kernel,
        grid=(n // tm,),
        in_specs=[zb(Z_AQ, ATT_W), zb(Z_AK, ATT_W), zb(Z_AV, ATT_W), zb(Z_IQ, IDX_Q_W), zb(Z_MISC, LANES),
                  tb, tb, tb, tb, gb, gb],
        out_specs=[ob(ATT_W), ob(ATT_W), ob(ATT_W), ob(ATT_W), ob(IDX_Q_W), ob(LANES), ob(IDX_DH)],
        out_shape=[jax.ShapeDtypeStruct((n, ATT_W), f32), jax.ShapeDtypeStruct((n, ATT_W), f32),
                   jax.ShapeDtypeStruct((n, ATT_W), bf), jax.ShapeDtypeStruct((n, ATT_W), bf),
                   jax.ShapeDtypeStruct((n, IDX_Q_W), bf), jax.ShapeDtypeStruct((n, LANES), f32),
                   jax.ShapeDtypeStruct((n, IDX_DH), bf)],
        compiler_params=_cparams(("parallel",)),
        name="post",
    )(z, z, z, z, z, c128, s128, c64, s64, q_norm, k_norm)


def _rope_tables(pos):
    def tab(dh):
        half = dh // 2
        inv = ROPE_THETA ** (-jnp.arange(half, dtype=jnp.float32) / half)
        ang = pos.astype(jnp.float32)[:, None] * inv[None, :]
        c, s = jnp.cos(ang), jnp.sin(ang)
        reps = LANES // dh
        return (jnp.concatenate([c, c] * reps, axis=1), jnp.concatenate([-s, s] * reps, axis=1))
    c128, s128 = tab(ATT_DH)
    c64, s64 = tab(IDX_DH)
    return c128, s128, c64, s64


def _log_sigmoid(x):
    return jnp.minimum(x, 0.0) - jnp.log(1.0 + jnp.exp(-jnp.abs(x)))


def _gla_prompt_kernel(qk_ref, v_ref, gr_ref, misc_ref, a2_ref, ab_ref, gn_ref, o_ref, sfin_ref, s_scr, *, rows):
    t = pl.program_id(1)
    C = GLA_CHUNK
    hi = lax.Precision.HIGHEST

    @pl.when(t == 0)
    def _():
        s_scr[...] = jnp.zeros_like(s_scr)

    ri = lax.broadcasted_iota(jnp.int32, (C, C), 0)
    ci = lax.broadcasted_iota(jnp.int32, (C, C), 1)
    causal = ri >= ci
    tri = causal.astype(jnp.float32)
    ones = jnp.ones((C, GLA_DK), jnp.float32)
    gn = gn_ref[...]

    def chunk(c, carry):
        r0 = pl.multiple_of(c * C, C)
        rs = pl.ds(r0, C)
        glr = misc_ref[rs, MISC_GLR:MISC_GLR + GLA_GATE_RANK]
        pre_all = jnp.dot(glr, a2_ref[...], precision=hi, preferred_element_type=jnp.float32) + ab_ref[...]
        la_all = _log_sigmoid(pre_all) / GLA_TAU
        b_all = jnp.dot(tri, la_all, precision=hi, preferred_element_type=jnp.float32)
        blt_all = jnp.dot(la_all.T, ones, precision=hi, preferred_element_type=jnp.float32)
        for h in range(GLA_HEADS):
            ks = slice(h * GLA_DK, (h + 1) * GLA_DK)
            vs = slice(h * GLA_DV, (h + 1) * GLA_DV)
            q = qk_ref[rs, ks] * (GLA_DK ** -0.5)
            k = qk_ref[rs, GLA_QK_W + h * GLA_DK:GLA_QK_W + (h + 1) * GLA_DK]
            v = v_ref[rs, vs]
            b = b_all[:, ks]
            qd = (q * jnp.exp(b)).astype(MXU_DTYPE)
            kd = (k * jnp.exp(-b)).astype(MXU_DTYPE)
            vb = v.astype(MXU_DTYPE)
            a = lax.dot_general(qd, kd, (((1,), (1,)), ((), ())), preferred_element_type=jnp.float32)
            a = jnp.where(causal, a, 0.0)
            s_old = s_scr[h]
            o = (jnp.dot(a.astype(MXU_DTYPE), vb, preferred_element_type=jnp.float32)
                 + jnp.dot(qd, s_old.astype(MXU_DTYPE), preferred_element_type=jnp.float32))
            b_last = b[C - 1:C, :]
            kl_t = (k * jnp.exp(b_last - b)).T.astype(MXU_DTYPE)
            dec = jnp.exp(blt_all[ks, :])
            s_scr[h] = (jnp.concatenate([dec] * (GLA_DV // LANES), axis=1) * s_old
                        + jnp.dot(kl_t, vb, preferred_element_type=jnp.float32))
            ms = jnp.mean(o * o, axis=-1, keepdims=True)
            on = o * lax.rsqrt(ms + EPS) * gn
            g = gr_ref[rs, vs]
            o_ref[rs, vs] = (on * (g * jax.nn.sigmoid(g))).astype(o_ref.dtype)
        return carry

    lax.fori_loop(0, rows // C, chunk, 0)

    @pl.when(t == pl.num_programs(1) - 1)
    def _():
        sfin_ref[0] = s_scr[...]


def _gla_prompt(z, a2, ab, gn, batch, seq):
    rows = _row_tile(seq, 512)
    nt = seq // rows
    zb = lambda off, w: pl.BlockSpec((rows, w), lambda b, t: (b * nt + t, off // w))
    full = lambda a: pl.BlockSpec(a.shape, lambda b, t: (0,) * a.ndim)
    return pl.pallas_call(
        functools.partial(_gla_prompt_kernel, rows=rows),
        grid=(batch, nt),
        in_specs=[zb(Z_GQK, 2 * GLA_QK_W), zb(Z_GV, GLA_V_W), zb(Z_GR, GLA_V_W), zb(Z_MISC, LANES),
                  full(a2), full(ab), full(gn)],
        out_specs=[pl.BlockSpec((rows, GLA_V_W), lambda b, t: (b * nt + t, 0)),
                   pl.BlockSpec((1, GLA_HEADS, GLA_DK, GLA_DV), lambda b, t: (b, 0, 0, 0))],
        out_shape=[jax.ShapeDtypeStruct((batch * seq, GLA_V_W), MXU_DTYPE),
                   jax.ShapeDtypeStruct((batch, GLA_HEADS, GLA_DK, GLA_DV), jnp.float32)],
        scratch_shapes=[pltpu.VMEM((GLA_HEADS, GLA_DK, GLA_DV), jnp.float32)],
        compiler_params=_cparams(("parallel", "arbitrary")),
        name="gla_prompt",
    )(z, z, z, z, a2, ab, gn)


def _gla_sample_kernel(qk_ref, v_ref, gr_ref, misc_ref, a2_ref, ab_ref, gn_ref, s0_ref, o_ref, s1_ref):
    hi = lax.Precision.HIGHEST
    eye = (lax.broadcasted_iota(jnp.int32, (GLA_DK, GLA_DK), 0)
           == lax.broadcasted_iota(jnp.int32, (GLA_DK, GLA_DK), 1))

    def col(row):
        return jnp.sum(jnp.where(eye, row, 0.0), axis=1, keepdims=True)

    glr = misc_ref[0][:, MISC_GLR:MISC_GLR + GLA_GATE_RANK]
    qk, v, gr = qk_ref[0], v_ref[0], gr_ref[0]
    for h in range(GLA_HEADS):
        ks = slice(h * GLA_DK, (h + 1) * GLA_DK)
        vs = slice(h * GLA_DV, (h + 1) * GLA_DV)
        q = qk[:, ks] * (GLA_DK ** -0.5)
        k = qk[:, GLA_QK_W + h * GLA_DK:GLA_QK_W + (h + 1) * GLA_DK]
        pre = jnp.dot(glr, a2_ref[:, ks], precision=hi, preferred_element_type=jnp.float32) + ab_ref[:, ks]
        b = _log_sigmoid(pre) / GLA_TAU
        s1 = col(jnp.exp(b)) * s0_ref[0, h] + col(k) * v[:, vs]
        s1_ref[0, h] = s1
        o = jnp.sum(col(q) * s1, axis=0, keepdims=True)
        ms = jnp.mean(o * o, axis=-1, keepdims=True)
        on = o * lax.rsqrt(ms + EPS) * gn_ref[...]
        g = gr[:, vs]
        o_ref[0, :, vs] = (on * (g * jax.nn.sigmoid(g))).astype(o_ref.dtype)


def _gla_sample(z, a2, ab, gn, s0):
    n = z.shape[0]
    z3 = z.reshape(n, 1, Z_W)
    zb = lambda off, w: pl.BlockSpec((1, 1, w), lambda b: (b, 0, off // w))
    full = lambda a: pl.BlockSpec(a.shape, lambda b: (0,) * a.ndim)
    sb = pl.BlockSpec((1, GLA_HEADS, GLA_DK, GLA_DV), lambda b: (b, 0, 0, 0))
    o, s1 = pl.pallas_call(
        _gla_sample_kernel,
        grid=(n,),
        in_specs=[zb(Z_GQK, 2 * GLA_QK_W), zb(Z_GV, GLA_V_W), zb(Z_GR, GLA_V_W), zb(Z_MISC, LANES),
                  full(a2), full(ab), full(gn), sb],
        out_specs=[pl.BlockSpec((1, 1, GLA_V_W), lambda b: (b, 0, 0)), sb],
        out_shape=[jax.ShapeDtypeStruct((n, 1, GLA_V_W), MXU_DTYPE),
                   jax.ShapeDtypeStruct(s0.shape, jnp.float32)],
        compiler_params=_cparams(("parallel",)),
        name="gla_sample",
    )(z3, z3, z3, z3, a2, ab, gn, s0)
    return o.reshape(n, GLA_V_W), s1


def _sortable(x):
    x = jnp.where(x == 0.0, 0.0, x)
    bits = pltpu.bitcast(x, jnp.int32)
    return jnp.where(bits < 0, bits ^ jnp.int32(0x7FFFFFFF), bits)


def _idx_select_kernel(iq_ref, misc_ref, ik_ref, mask_ref, key_scr, *, tq, rg, ck, seq, topk):
    i32 = jnp.int32
    ng = tq // rg
    nsub = ck // LANES
    nblk = seq // LANES
    qi = pl.program_id(1)
    nck = ((qi + 1) * tq + ck - 1) // ck
    lane = lax.broadcasted_iota(i32, (rg, LANES), 1)

    def slab(g, c, u):
        return key_scr.at[g * nblk + c * nsub + u]

    sg = min(tq, 2 * rg)
    for g in range(tq // sg):
        rows = slice(g * sg, (g + 1) * sg)
        row = qi * tq + g * sg + lax.broadcasted_iota(i32, (sg, 1), 0)
        w = misc_ref[rows, MISC_IW:MISC_IW + IDX_HEADS]
        iq = iq_ref[rows, :]
        iqh = [iq[:, h * IDX_DH:(h + 1) * IDX_DH] for h in range(IDX_HEADS)]
        wh = [w[:, h:h + 1] for h in range(IDX_HEADS)]

        def score_chunk(c, carry, g=g, row=row, iqh=iqh, wh=wh):
            c0 = pl.multiple_of(c * ck, ck)
            ks = ik_ref[pl.ds(c0, ck), :]
            acc = jnp.zeros((sg, ck), jnp.float32)
            for h in range(IDX_HEADS):
                s = lax.dot_general(iqh[h], ks, (((1,), (1,)), ((), ())), preferred_element_type=jnp.float32)
                acc = acc + wh[h] * jnp.maximum(s, 0.0)
            col = c0 + lax.broadcasted_iota(i32, (1, ck), 1)
            key = jnp.where(col <= row, _sortable(acc), INT_MIN)
            for k in range(sg // rg):
                for u in range(nsub):
                    slab(g * (sg // rg) + k, c, u)[...] = key[k * rg:(k + 1) * rg, u * LANES:(u + 1) * LANES]
            return carry

        lax.fori_loop(0, nck, score_chunk, 0)

    ones_mx = jnp.ones((LANES, LANES), MXU_DTYPE)

    def count(pred):
        accs = []
        for g in range(ng):
            def body(c, acc, g=g):
                c0 = pl.multiple_of(c * ck, ck)
                for u in range(nsub):
                    acc = acc + pred(g, slab(g, c, u)[...], lane + (c0 + u * LANES))
                return acc
            accs.append(lax.fori_loop(0, nck, body, jnp.zeros((rg, LANES), i32)))
        return [jnp.dot(a.astype(jnp.float32).astype(MXU_DTYPE), ones_mx, preferred_element_type=jnp.float32)
                for a in accs]

    def bit_step(i, ans):
        bit = lax.shift_left(i32(1), i32(31) - i)
        cand = [a ^ bit for a in ans]
        cnt = count(lambda g, blk, col: jnp.where(blk >= cand[g], 1, 0))
        return tuple(jnp.where(cnt[g] >= topk, cand[g], ans[g]) for g in range(ng))

    ans = lax.fori_loop(0, 32, bit_step, tuple(jnp.full((rg, LANES), INT_MIN, i32) for _ in range(ng)))
    tb = [jnp.maximum(a, INT_MIN + 1) for a in ans]

    c_gt = count(lambda g, blk, col: jnp.where(blk > tb[g], 1, 0))
    c_eq = count(lambda g, blk, col: jnp.where(blk == tb[g], 1, 0))
    need = [topk - c for c in c_gt]
    excess = [c_eq[g] > need[g] for g in range(ng)]
    n_excess = sum(jnp.max(jnp.where(e, 1, 0)) for e in excess)

    def write(select):
        def write_chunk(c, carry):
            c0 = pl.multiple_of(c * ck, ck)
            for g in range(ng):
                for u in range(nsub):
                    sel = select(g, slab(g, c, u)[...], lane + (c0 + u * LANES))
                    mask_ref[g * rg:(g + 1) * rg, pl.ds(c0 + u * LANES, LANES)] = sel.astype(mask_ref.dtype)
            return carry
        lax.fori_loop(0, nck, write_chunk, 0)

    @pl.when(n_excess == 0)
    def _():
        write(lambda g, blk, col: jnp.where(blk >= tb[g], 1, 0))

    @pl.when(n_excess > 0)
    def _():
        nbits = max(1, (seq - 1).bit_length())

        def jbit(i, jv):
            bit = lax.shift_left(i32(1), i32(nbits - 1) - i)
            cand = [j | bit for j in jv]
            gcnt = count(lambda g, blk, col: jnp.where(blk == tb[g], jnp.where(col < cand[g], 1, 0), 0))
            return tuple(jnp.where(gcnt[g] < need[g], cand[g], jv[g]) for g in range(ng))

        jv = lax.fori_loop(0, nbits, jbit, tuple(jnp.zeros((rg, LANES), i32) for _ in range(ng)))
        jb = [jnp.where(excess[g], jv[g], seq) for g in range(ng)]
        write(lambda g, blk, col: jnp.where(blk > tb[g], 1,
                                            jnp.where(blk == tb[g], jnp.where(col <= jb[g], 1, 0), 0)))

    zeros8 = jnp.zeros((tq, ck), mask_ref.dtype)

    def zero_chunk(c, carry):
        mask_ref[:, pl.ds(pl.multiple_of(c * ck, ck), ck)] = zeros8
        return carry

    lax.fori_loop(nck, seq // ck, zero_chunk, 0)


def _idx_select(iqb, misc2, ikb, batch, seq, topk):
    tq = _row_tile(seq, 512)
    rg = _row_tile(tq, 128)
    ck = _row_tile(seq, 512)
    nq = seq // tq
    return pl.pallas_call(
        functools.partial(_idx_select_kernel, tq=tq, rg=rg, ck=ck, seq=seq, topk=topk),
        grid=(batch, nq),
        in_specs=[pl.BlockSpec((tq, IDX_Q_W), lambda b, q: (b * nq + q, 0)),
                  pl.BlockSpec((tq, LANES), lambda b, q: (b * nq + q, 0)),
                  pl.BlockSpec((seq, IDX_DH), lambda b, q: (b, 0))],
        out_specs=pl.BlockSpec((tq, seq), lambda b, q: (b * nq + q, 0)),
        out_shape=jax.ShapeDtypeStruct((batch * seq, seq), jnp.int8),
        scratch_shapes=[pltpu.VMEM(((tq // rg) * (seq // LANES), rg, LANES), jnp.int32)],
        compiler_params=_cparams(("parallel", "arbitrary")),
        name="idx_select",
    )(iqb, misc2, ikb)


def _attn_kernel(q_ref, k_ref, v_ref, m_ref, o_ref, qb_scr, acc_scr, m_scr, l_scr):
    qi, ki = pl.program_id(1), pl.program_id(2)
    t, tk = q_ref.shape[0], k_ref.shape[0]
    nsub = tk // LANES

    @pl.when(ki == 0)
    def _():
        qb_scr[...] = (q_ref[...] * (ATT_DH ** -0.5 * math.log2(math.e))).astype(qb_scr.dtype)
        acc_scr[...] = jnp.zeros_like(acc_scr)
        m_scr[...] = jnp.full(m_scr.shape, NEG, jnp.float32)
        l_scr[...] = jnp.zeros_like(l_scr)

    @pl.when(ki <= qi)
    def _():
        bias = jnp.where(m_ref[...].astype(jnp.int32) != 0, 0.0, NEG)
        ones = jnp.ones((tk, LANES), v_ref.dtype)
        for h in range(ATT_HEADS):
            sl = slice(h * ATT_DH, (h + 1) * ATT_DH)
            s = lax.dot_general(qb_scr[:, sl], k_ref[:, sl], (((1,), (1,)), ((), ())),
                                preferred_element_type=jnp.float32) + bias
            mx = s[:, 0:LANES]
            for u in range(1, nsub):
                mx = jnp.maximum(mx, s[:, u * LANES:(u + 1) * LANES])
            m_prev = m_scr[h]
            m_new = jnp.maximum(m_prev, jnp.max(mx, axis=1, keepdims=True))
            alpha = jnp.exp2(m_prev - m_new)
            p = jnp.exp2(s - jnp.concatenate([m_new] * nsub, axis=1)).astype(v_ref.dtype)
            pv = jnp.dot(p, jnp.concatenate([v_ref[:, sl], ones], axis=1), preferred_element_type=jnp.float32)
            m_scr[h] = m_new
            l_scr[h] = alpha * l_scr[h] + pv[:, ATT_DH:]
            acc_scr[:, sl] = alpha * acc_scr[:, sl] + pv[:, :ATT_DH]

    @pl.when(ki == qi)
    def _():
        for h in range(ATT_HEADS):
            sl = slice(h * ATT_DH, (h + 1) * ATT_DH)
            o_ref[:, sl] = (acc_scr[:, sl] / l_scr[h]).astype(o_ref.dtype)


def _attn(q, kb, vb, mask, batch, seq):
    t = _row_tile(seq, 512)
    nt = seq // t
    return pl.pallas_call(
        _attn_kernel,
        grid=(batch, nt, nt),
        in_specs=[pl.BlockSpec((t, ATT_W), lambda b, i, j: (b * nt + i, 0)),
                  pl.BlockSpec((t, ATT_W), lambda b, i, j: (b * nt + jnp.minimum(i, j), 0)),
                  pl.BlockSpec((t, ATT_W), lambda b, i, j: (b * nt + jnp.minimum(i, j), 0)),
                  pl.BlockSpec((t, t), lambda b, i, j: (b * nt + i, jnp.minimum(i, j)))],
        out_specs=pl.BlockSpec((t, ATT_W), lambda b, i, j: (b * nt + i, 0)),
        out_shape=jax.ShapeDtypeStruct((batch * seq, ATT_W), MXU_DTYPE),
        scratch_shapes=[pltpu.VMEM((t, ATT_W), MXU_DTYPE), pltpu.VMEM((t, ATT_W), jnp.float32),
                        pltpu.VMEM((ATT_HEADS, t, LANES), jnp.float32),
                        pltpu.VMEM((ATT_HEADS, t, LANES), jnp.float32)],
        compiler_params=_cparams(("parallel", "parallel", "arbitrary")),
        name="attn",
    )(q, kb, vb, mask)


def _sidx_kernel(pt_ref, iq_ref, w_ref, ikn_ref, cache_ref, out_ref, buf, sem, *, n_pages):
    b = pl.program_id(0)
    nb = pl.num_programs(0)
    past = n_pages * PAGE_SIZE

    def page_copy(bb, slot, p):
        return pltpu.make_async_copy(cache_ref.at[pt_ref[bb, p]],
                                     buf.at[slot, :, pl.ds(p * PAGE_SIZE, PAGE_SIZE)], sem.at[slot])

    def start(bb, slot):
        for p in range(n_pages):
            page_copy(bb, slot, p).start()

    @pl.when(b == 0)
    def _():
        start(0, 0)

    @pl.when(b + 1 < nb)
    def _():
        start(b + 1, (b + 1) % 2)

    slot = b % 2
    for p in range(n_pages):
        page_copy(b, slot, p).wait()

    iq = iq_ref[0]
    w = w_ref[0]
    ikt = buf[slot].astype(iq.dtype)
    s = jnp.dot(iq, ikt, preferred_element_type=jnp.float32)
    sc = jnp.sum(w * jnp.maximum(s, 0.0), axis=0, keepdims=True)
    out_ref[0, :, 0:past] = jnp.where(sc == 0.0, 0.0, sc)
    s_new = jnp.sum(iq.astype(jnp.float32) * ikn_ref[0].astype(iq.dtype).astype(jnp.float32), axis=1, keepdims=True)
    sc_new = jnp.sum(w * jnp.maximum(s_new, 0.0), axis=0, keepdims=True)
    sc_new = jnp.where(sc_new == 0.0, 0.0, sc_new)
    lane = lax.broadcasted_iota(jnp.int32, (1, LANES), 1)
    out_ref[0, :, past:past + LANES] = jnp.where(lane == 0, sc_new, -jnp.inf)


def _sidx(page_table, iq3, w3, ikn3, cache_idx):
    bd, n_pages = page_table.shape
    past = n_pages * PAGE_SIZE
    grid_spec = pltpu.PrefetchScalarGridSpec(
        num_scalar_prefetch=1,
        grid=(bd,),
        in_specs=[pl.BlockSpec((1, IDX_HEADS, IDX_DH), lambda b, pt: (b, 0, 0)),
                  pl.BlockSpec((1, IDX_HEADS, 1), lambda b, pt: (b, 0, 0)),
                  pl.BlockSpec((1, 1, IDX_DH), lambda b, pt: (b, 0, 0)),
                  pl.BlockSpec(memory_space=pl.ANY)],
        out_specs=pl.BlockSpec((1, 1, past + LANES), lambda b, pt: (b, 0, 0)),
        scratch_shapes=[pltpu.VMEM((2, IDX_DH, past), jnp.float32), pltpu.SemaphoreType.DMA((2,))],
    )
    return pl.pallas_call(
        functools.partial(_sidx_kernel, n_pages=n_pages),
        grid_spec=grid_spec,
        out_shape=jax.ShapeDtypeStruct((bd, 1, past + LANES), jnp.float32),
        compiler_params=_cparams(("arbitrary",)),
        name="sidx",
    )(page_table, iq3, w3, ikn3, cache_idx)


def _stopk_kernel(sc_ref, sel_ref, x_scr, *, topk):
    x_scr[...] = sc_ref[...]
    shape = x_scr.shape
    big = jnp.int32(2 ** 30)

    def body(it, sel):
        x = x_scr[...]
        col = lax.broadcasted_iota(jnp.int32, shape, 1)
        m = jnp.max(x, axis=1, keepdims=True)
        idx = jnp.min(jnp.where(x == m, col, big), axis=1, keepdims=True)
        x_scr[...] = jnp.where(col == idx, -jnp.inf, x)
        kcol = lax.broadcasted_iota(jnp.int32, sel.shape, 1)
        return jnp.where(kcol == it, idx, sel)

    sel_ref[...] = lax.fori_loop(0, topk, body, jnp.zeros(sel_ref.shape, jnp.int32))


def _stopk(scores, topk):
    bd, lp = scores.shape
    return pl.pallas_call(
        functools.partial(_stopk_kernel, topk=topk),
        out_shape=jax.ShapeDtypeStruct((bd, topk), jnp.int32),
        scratch_shapes=[pltpu.VMEM((bd, lp), jnp.float32)],
        compiler_params=pltpu.CompilerParams(vmem_limit_bytes=VMEM_LIMIT),
        name="stopk",
    )(scores)


def _sattn_kernel(sel_ref, pt_ref, q_ref, kn_ref, vn_ref, selv_ref, ck_ref, cv_ref, o_ref, kbuf, vbuf, sem,
                  *, topk, past):
    b = pl.program_id(0)
    nb = pl.num_programs(0)

    def row_copies(bb, slot, j):
        p = jnp.minimum(sel_ref[bb, j], past - 1)
        phys = pt_ref[bb, p // PAGE_SIZE]
        r = p % PAGE_SIZE
        return (pltpu.make_async_copy(ck_ref.at[phys, r], kbuf.at[slot, j], sem.at[0, slot]),
                pltpu.make_async_copy(cv_ref.at[phys, r], vbuf.at[slot, j], sem.at[1, slot]))

    def start(bb, slot):
        def body(j, c):
            ck, cv = row_copies(bb, slot, j)
            ck.start()
            cv.start()
            return c
        lax.fori_loop(0, topk, body, 0)

    @pl.when(b == 0)
    def _():
        start(0, 0)

    @pl.when(b + 1 < nb)
    def _():
        start(b + 1, (b + 1) % 2)

    slot = b % 2

    def wbody(j, c):
        ck, cv = row_copies(b, slot, j)
        ck.wait()
        cv.wait()
        return c

    lax.fori_loop(0, topk, wbody, 0)

    selv = selv_ref[0]
    in_past = selv < past
    q = q_ref[...]
    kg = jnp.where(in_past, kbuf[slot], kn_ref[...])
    vg = jnp.where(in_past, vbuf[slot], vn_ref[...])
    s = jnp.sum(kg * q, axis=-1, keepdims=True) * (ATT_DH ** -0.5)
    s = jnp.where(selv <= past, s, -jnp.inf)
    m = jnp.max(s, axis=0, keepdims=True)
    p = jnp.exp(s - m)
    p = p / jnp.sum(p, axis=0, keepdims=True)
    o_ref[...] = jnp.sum(p * vg, axis=0, keepdims=True).astype(o_ref.dtype)


def _sattn(sel, page_table, q3, kn3, vn3, cache_k, cache_v):
    bd, topk = sel.shape
    past = page_table.shape[1] * PAGE_SIZE
    hb = pl.BlockSpec((1, ATT_HEADS, ATT_DH), lambda b, s, pt: (b, 0, 0))
    grid_spec = pltpu.PrefetchScalarGridSpec(
        num_scalar_prefetch=2,
        grid=(bd,),
        in_specs=[hb, hb, hb, pl.BlockSpec((1, topk, 1, 1), lambda b, s, pt: (b, 0, 0, 0)),
                  pl.BlockSpec(memory_space=pl.ANY), pl.BlockSpec(memory_space=pl.ANY)],
        out_specs=hb,
        scratch_shapes=[pltpu.VMEM((2, topk, ATT_HEADS, ATT_DH), jnp.float32),
                        pltpu.VMEM((2, topk, ATT_HEADS, ATT_DH), jnp.float32),
                        pltpu.SemaphoreType.DMA((2, 2))],
    )
    return pl.pallas_call(
        functools.partial(_sattn_kernel, topk=topk, past=past),
        grid_spec=grid_spec,
        out_shape=jax.ShapeDtypeStruct((bd, ATT_HEADS, ATT_DH), MXU_DTYPE),
        compiler_params=_cparams(("arbitrary",)),
        name="sattn",
    )(sel, page_table, q3, kn3, vn3, sel.reshape(bd, topk, 1, 1), cache_k, cache_v)


def _merge_kernel(og_ref, oa_ref, ga_ref, gb_ref, x_ref, wg_ref, wa_ref, wo_ref, nf_ref, x1_ref, hn_ref):
    f32 = jnp.float32
    mix = (jax.nn.sigmoid(ga_ref[...]) * jnp.dot(og_ref[...], wg_ref[...], preferred_element_type=f32)
           + jax.nn.sigmoid(gb_ref[...]) * jnp.dot(oa_ref[...], wa_ref[...], preferred_element_type=f32))
    x1 = x_ref[...] + jnp.dot(mix.astype(wo_ref.dtype), wo_ref[...], preferred_element_type=f32)
    x1_ref[...] = x1
    ms = jnp.mean(x1 * x1, axis=-1, keepdims=True)
    hn_ref[...] = (x1 * lax.rsqrt(ms + EPS) * nf_ref[...]).astype(hn_ref.dtype)


def _merge(og, oa, z, x, wg, wa, wo, nf):
    n = x.shape[0]
    tm = _row_tile(n, 512)
    rb = lambda: pl.BlockSpec((tm, D_MODEL), lambda i: (i, 0))
    zb = lambda off: pl.BlockSpec((tm, D_MODEL), lambda i: (i, off // D_MODEL))
    wb = lambda: pl.BlockSpec((D_MODEL, D_MODEL), lambda i: (0, 0))
    return pl.pallas_call(
        _merge_kernel,
        grid=(n // tm,),
        in_specs=[rb(), rb(), zb(Z_GA), zb(Z_GB), rb(), wb(), wb(), wb(), pl.BlockSpec((1, D_MODEL), lambda i: (0, 0))],
        out_specs=[rb(), rb()],
        out_shape=[jax.ShapeDtypeStruct((n, D_MODEL), jnp.float32), jax.ShapeDtypeStruct((n, D_MODEL), MXU_DTYPE)],
        compiler_params=_cparams(("parallel",)),
        name="merge",
    )(og, oa, z, z, x, wg, wa, wo, nf)


_PAIRS = [(r1, r2) for r1 in range(PEER_TOPK) for r2 in range(PEER_TOPK) if (r1 + 1) * (r2 + 1) <= PEER_TOPK]


def _top16_sorted(x, exact_ties):
    nk = x.shape[0]
    rid = lax.broadcasted_iota(jnp.int32, x.shape, 0)
    rank = jnp.full(x.shape, PEER_TOPK, jnp.int32)
    vals = []
    for r in range(PEER_TOPK):
        m = jnp.max(x, axis=0, keepdims=True)
        hit = x == m
        if exact_ties:
            hit = rid == jnp.min(jnp.where(hit, rid, nk), axis=0, keepdims=True)
        x = jnp.where(hit, -jnp.inf, x)
        rank = jnp.where(hit, r, rank)
        vals.append(m)
    return vals, rank


def _router_body(hn_ref, wqt_ref, keys_ref, r2_ref, e2_ref, cnt_ref, e1_ref, exact_ties):
    f32 = jnp.float32
    qt = lax.dot_general(wqt_ref[...], hn_ref[...], (((1,), (1,)), ((), ())), preferred_element_type=f32)
    qt = qt.astype(MXU_DTYPE)
    half = PEER_DKEY // 2
    npair = len(_PAIRS)
    ok = jnp.zeros((1, hn_ref.shape[0]), jnp.int32)
    for h in range(PEER_HEADS):
        s1 = jnp.dot(keys_ref[2 * h], qt[(2 * h) * half:(2 * h + 1) * half, :], preferred_element_type=f32)
        s2 = jnp.dot(keys_ref[2 * h + 1], qt[(2 * h + 1) * half:(2 * h + 2) * half, :], preferred_element_type=f32)
        v1, rank1 = _top16_sorted(s1, exact_ties)
        v2, rank2 = _top16_sorted(s2, exact_ties)
        sel1 = rank1 < PEER_TOPK
        sel2 = rank2 < PEER_TOPK
        n12 = (jnp.sum(jnp.where(sel1, 1, 0), axis=0, keepdims=True)
               + jnp.sum(jnp.where(sel2, 1, 0), axis=0, keepdims=True))
        ok = jnp.where(n12 == 2 * PEER_TOPK, ok, -1)
        cand = jnp.concatenate([v1[a] + v2[b] for a, b in _PAIRS], axis=0)
        pid = lax.broadcasted_iota(jnp.int32, cand.shape, 0)
        c = cand
        picked = jnp.zeros(cand.shape, jnp.int32)
        for _ in range(PEER_TOPK):
            m = jnp.max(c, axis=0, keepdims=True)
            hit = pid == jnp.min(jnp.where(c == m, pid, npair), axis=0, keepdims=True)
            c = jnp.where(hit, -jnp.inf, c)
            picked = jnp.where(hit, 1, picked)
        top = v1[0] + v2[0]
        zsum = jnp.sum(jnp.where(picked > 0, jnp.exp(cand - top), 0.0), axis=0, keepdims=True)
        half_inv_z = 0.5 / zsum
        cnt = jnp.zeros(s1.shape, f32)
        for a in range(PEER_TOPK):
            rows = [i for i, (pa, _) in enumerate(_PAIRS) if pa == a]
            n_a = jnp.sum(picked[rows[0]:rows[-1] + 1], axis=0, keepdims=True).astype(f32)
            cnt = jnp.where(rank1 == a, n_a, cnt)
        cnt_ref[0, h] = cnt
        e1_ref[0, h] = jnp.where(sel1, jnp.exp(s1 - v1[0]) * half_inv_z, 0.0)
        r2_ref[0, h] = rank2.astype(f32).astype(r2_ref.dtype)
        e2_ref[0, h] = jnp.where(sel2, jnp.exp(s2 - v2[0]), 0.0).astype(e2_ref.dtype)
    return ok


def _router_kernel(hn_ref, wqt_ref, keys_ref, r2_ref, e2_ref, cnt_ref, e1_ref):
    outs = (r2_ref, e2_ref, cnt_ref, e1_ref)
    ok = _router_body(hn_ref, wqt_ref, keys_ref, *outs, exact_ties=False)

    @pl.when(jnp.min(ok) < 0)
    def _():
        _router_body(hn_ref, wqt_ref, keys_ref, *outs, exact_ties=True)


def _router(hn, wqt, keys, tm):
    n = hn.shape[0]
    nt = n // tm
    tr = min(tm, LANES)
    per = tm // tr
    shp = lambda dt: jax.ShapeDtypeStruct((nt, PEER_HEADS, PEER_NKEYS, tm), dt)
    ob = pl.BlockSpec((1, PEER_HEADS, PEER_NKEYS, tr), lambda i: (i // per, 0, 0, i % per))
    return pl.pallas_call(
        _router_kernel,
        grid=(n // tr,),
        in_specs=[pl.BlockSpec((tr, D_MODEL), lambda i: (i, 0)),
                  pl.BlockSpec((D_MODEL, D_MODEL), lambda i: (0, 0)),
                  pl.BlockSpec(keys.shape, lambda i: (0, 0, 0))],
        out_specs=[ob, ob, ob, ob],
        out_shape=[shp(GATE_DTYPE), shp(GATE_DTYPE), shp(jnp.float32), shp(jnp.float32)],
        compiler_params=_cparams(("parallel",)),
        name="router",
    )(hn, wqt, keys)


def _gelu2(x):
    return x * (1.0 + lax.erf(x * (2.0 ** -0.5)))


def _peer_kernel(hn_ref, u_ref, vt_ref, r2_ref, e2_ref, cnt_ref, e1_ref, x1_ref, y_ref, acc_scr, p_scr, *, rows1):
    j = pl.program_id(1)
    f32 = jnp.float32
    gd = r2_ref.dtype
    tm = hn_ref.shape[0]
    sub = 16
    grp = PEER_NKEYS // sub

    @pl.when(j == 0)
    def _():
        acc_scr[...] = jnp.zeros_like(acc_scr)

    at = lax.dot_general(u_ref[...], hn_ref[...], (((1,), (1,)), ((), ())), preferred_element_type=f32)
    for c in range(rows1):
        w = jnp.zeros((grp, sub, tm), gd)
        for h in range(PEER_HEADS):
            cnt = jnp.broadcast_to(cnt_ref[0, h, c:c + 1, :], (sub, tm)).astype(gd)[None]
            e1 = jnp.broadcast_to(e1_ref[0, h, c:c + 1, :], (sub, tm)).astype(gd)[None]
            r2 = r2_ref[0, h].reshape(grp, sub, tm)
            e2 = e2_ref[0, h].reshape(grp, sub, tm)
            w = w + jnp.where(r2 < cnt, e2, jnp.zeros_like(e2)) * e1
        rs = slice(c * PEER_NKEYS, (c + 1) * PEER_NKEYS)
        p_scr[rs, :] = w.reshape(PEER_NKEYS, tm) * _gelu2(at[rs, :]).astype(gd)
    acc_scr[...] += jnp.dot(vt_ref[...], p_scr[...], preferred_element_type=f32)

    @pl.when(j == pl.num_programs(1) - 1)
    def _():
        y_ref[...] = x1_ref[...] + acc_scr[...].T


def _peer(hn, u, vt, router_out, x1, tm):
    n = hn.shape[0]
    nt = n // tm
    rows1 = 16
    te = rows1 * PEER_NKEYS
    ne = (PEER_NKEYS * PEER_NKEYS) // te
    r2, e2, cnt, e1 = router_out
    full = pl.BlockSpec((1, PEER_HEADS, PEER_NKEYS, tm), lambda i, j: (i, 0, 0, 0))
    part = pl.BlockSpec((1, PEER_HEADS, rows1, tm), lambda i, j: (i, 0, j, 0))
    return pl.pallas_call(
        functools.partial(_peer_kernel, rows1=rows1),
        grid=(nt, ne),
        in_specs=[pl.BlockSpec((tm, D_MODEL), lambda i, j: (i, 0)),
                  pl.BlockSpec((te, D_MODEL), lambda i, j: (j, 0)),
                  pl.BlockSpec((D_MODEL, te), lambda i, j: (0, j)),
                  full, full, part, part,
                  pl.BlockSpec((tm, D_MODEL), lambda i, j: (i, 0))],
        out_specs=pl.BlockSpec((tm, D_MODEL), lambda i, j: (i, 0)),
        out_shape=jax.ShapeDtypeStruct((n, D_MODEL), jnp.float32),
        scratch_shapes=[pltpu.VMEM((D_MODEL, tm), jnp.float32), pltpu.VMEM((te, tm), MXU_DTYPE)],
        compiler_params=_cparams(("parallel", "arbitrary")),
        name="peer",
    )(hn, u, vt, r2, e2, cnt, e1, x1)


def _transpose_cast_kernel(x_ref, o_ref):
    o_ref[...] = x_ref[...].T.astype(o_ref.dtype)


def _transpose_cast(x, dtype):
    r, c = x.shape
    tr = _row_tile(r, 512)
    return pl.pallas_call(
        _transpose_cast_kernel,
        grid=(r // tr,),
        in_specs=[pl.BlockSpec((tr, c), lambda i: (i, 0))],
        out_specs=pl.BlockSpec((c, tr), lambda i: (0, i)),
        out_shape=jax.ShapeDtypeStruct((c, r), dtype),
        compiler_params=_cparams(("parallel",)),
        name="transpose_cast",
    )(x)


def _reorder_w_in(w_in):
    sizes = (GLA_QK_W, GLA_QK_W, GLA_V_W, GLA_V_W, GLA_GATE_RANK, ATT_W, ATT_W, ATT_W, IDX_Q_W, IDX_DH, IDX_HEADS,
             D_MODEL, D_MODEL)
    offs = [0]
    for s in sizes:
        offs.append(offs[-1] + s)
    gq, gk, gv, gr, glr, aq, ak, av, iq, ik, iw, ga, gb = (w_in[:, offs[i]:offs[i + 1]] for i in range(len(sizes)))
    pad = jnp.zeros((w_in.shape[0], LANES - IDX_DH - IDX_HEADS - GLA_GATE_RANK), w_in.dtype)
    return jnp.concatenate([gq, gk, gv, gr, aq, ak, av, ga, gb, iq, ik, iw, glr, pad], axis=1)


def _ffn(og, oa, z, x, wg, wa, wo, nf, wqt, keys, u, vt):
    n = x.shape[0]
    x1, hn = _merge(og, oa, z, x, wg, wa, wo, nf)
    tm = _row_tile(n, 512)
    r = _router(hn, wqt, keys, tm)
    return _peer(hn, u, vt, r, x1, tm)


def kernel(x_prompt, x_sample, cache_k, cache_v, cache_idx_k, state_gla, page_table, norm_mix, w_in, gla_a2,
           gla_a_bias, q_norm, k_norm, gla_norm, w_br_gla, w_br_att, w_out, norm_ffn, peer_wq, peer_keys,
           peer_u, peer_v):
    depth = w_in.shape[0]
    assert depth == 1 and x_sample.shape[1] == 1
    B, S, _ = x_prompt.shape
    Bd = x_sample.shape[0]
    n_pages = page_table.shape[1]
    past = n_pages * PAGE_SIZE
    bf = MXU_DTYPE
    l = 0

    w_all = _reorder_w_in(w_in[l]).astype(bf)
    g_mix = norm_mix[l].reshape(1, D_MODEL)
    a2, ab = gla_a2[l], gla_a_bias[l].reshape(1, GLA_QK_W)
    gn = gla_norm[l].reshape(1, GLA_DV)
    qn, kn = q_norm[l].reshape(1, ATT_DH), k_norm[l].reshape(1, ATT_DH)
    wg, wa, wo = w_br_gla[l].astype(bf), w_br_att[l].astype(bf), w_out[l].astype(bf)
    nf = norm_ffn[l].reshape(1, D_MODEL)
    wqt = _transpose_cast(peer_wq[l], bf)
    keys = peer_keys[l].reshape(PEER_HEADS * 2, PEER_NKEYS, PEER_DKEY // 2).astype(bf)
    u = peer_u[l].astype(bf)
    vt = _transpose_cast(peer_v[l], bf)
    ffn_w = (wg, wa, wo, nf, wqt, keys, u, vt)

    xp = x_prompt.reshape(B * S, D_MODEL)
    zp = _proj(xp, g_mix, w_all)
    tabs_p = _rope_tables(jnp.arange(S))
    qp, kp, kpb, vpb, iqpb, miscp, ikpb = _post(zp, tabs_p, S, qn, kn)
    ogp, sp = _gla_prompt(zp, a2, ab, gn, B, S)
    topk_p = min(TOPK_MAX, S // 4)
    maskp = _idx_select(iqpb, miscp, ikpb, B, S, topk_p)
    oap = _attn(qp, kpb, vpb, maskp, B, S)
    yp = _ffn(ogp, oap, zp, xp, *ffn_w)

    xs = x_sample.reshape(Bd, D_MODEL)
    zs = _proj(xs, g_mix, w_all)
    tabs_s = _rope_tables(jnp.full((Bd,), past, jnp.int32))
    qs, ks, _, _, iqsb, miscs, iksb = _post(zs, tabs_s, Bd, qn, kn)
    vs = zs[:, Z_AV:Z_AV + ATT_W]
    ogs, ss = _gla_sample(zs, a2, ab, gn, state_gla[l])
    scores = _sidx(page_table, iqsb.reshape(Bd, IDX_HEADS, IDX_DH),
                   miscs[:, MISC_IW:MISC_IW + IDX_HEADS].reshape(Bd, IDX_HEADS, 1),
                   iksb.reshape(Bd, 1, IDX_DH), jnp.swapaxes(cache_idx_k[l], 1, 2))
    topk_s = min(TOPK_MAX, (past + 1) // 4)
    sel = _stopk(scores.reshape(Bd, past + LANES), topk_s)
    hd = (Bd, ATT_HEADS, ATT_DH)
    oas = _sattn(sel, page_table, qs.reshape(hd), ks.reshape(hd), vs.reshape(hd), cache_k[l], cache_v[l])
    ys = _ffn(ogs, oas.reshape(Bd, ATT_W), zs, xs, *ffn_w)

    return (yp.reshape(B, S, D_MODEL), ys.reshape(Bd, 1, D_MODEL),
            kp.reshape(1, B, S, ATT_HEADS, ATT_DH), zp[:, Z_AV:Z_AV + ATT_W].reshape(1, B, S, ATT_HEADS, ATT_DH),
            miscp[:, :IDX_DH].reshape(1, B, S, IDX_DH), sp[None],
            ks.reshape(1, Bd, 1, ATT_HEADS, ATT_DH), vs.reshape(1, Bd, 1, ATT_HEADS, ATT_DH),
            miscs[:, :IDX_DH].reshape(1, Bd, 1, IDX_DH), ss[None])
```
